```python
import jax, jax.numpy as jnp
from jax import lax
import numpy as np

D_MODEL = 1024
BATCH = 8
SEQ = 2048
DEPTH = 2
DEC_BATCH = 128
DEC_SEQ = 8
PAST_LEN = 16384
PAGE_SIZE = 128

D_MIX = D_MODEL
RG_WIDTH = D_MIX // 2
RG_BLOCKS = 8
RG_BLOCK = RG_WIDTH // RG_BLOCKS
RG_C = 8.0
DN_HEADS = 4
DN_HEAD_DIM = (D_MIX - RG_WIDTH) // DN_HEADS
DN_WIDTH = DN_HEADS * DN_HEAD_DIM
CONV_WIDTH = 4
CONV_CH = RG_WIDTH + 3 * DN_WIDTH
CHUNK = 64
D_FF = 2816
EPS = 1e-6
IN_COLS = 2 * RG_WIDTH + 4 * DN_WIDTH + 2 * DN_HEADS

kernel_name = 'hymba_rglru_gdn_macaron_step'


def rmsnorm(x, g):
    xf = x.astype(jnp.float32)
    y = xf * lax.rsqrt(jnp.mean(xf * xf, axis=-1, keepdims=True) + EPS)
    return (y * g.astype(jnp.float32)).astype(x.dtype)


def l2norm(x):
    return x * lax.rsqrt(jnp.sum(x * x, axis=-1, keepdims=True) + EPS)


def swiglu(h, w_up, w_down):
    u = h @ w_up
    gate, up = u[..., :D_FF], u[..., D_FF:]
    return (jax.nn.silu(gate) * up) @ w_down


def causal_conv(x, buf, w):
    L = x.shape[1]
    xp = jnp.concatenate([buf.astype(x.dtype), x], axis=1)
    out = sum(xp[:, j:j + L] * w[j] for j in range(CONV_WIDTH))
    return out, xp[:, -(CONV_WIDTH - 1):]


def _lin_combine(c1, c2):
    a1, b1 = c1
    a2, b2 = c2
    return a1 * a2, a2 * b1 + b2


def rg_lru(xr, h0, w_a, b_a, w_x, b_x, lam):
    B, L, _ = xr.shape
    xb = xr.reshape(B, L, RG_BLOCKS, RG_BLOCK)
    r = jax.nn.sigmoid(jnp.einsum('blhi,hij->blhj', xb, w_a).reshape(B, L, RG_WIDTH) + b_a)
    i = jax.nn.sigmoid(jnp.einsum('blhi,hij->blhj', xb, w_x).reshape(B, L, RG_WIDTH) + b_x)
    log_a = -RG_C * r * jax.nn.softplus(-lam.astype(jnp.float32))
    a = jnp.exp(log_a)
    b = jnp.sqrt(-jnp.expm1(2.0 * log_a)) * (i * xr)
    b = b.at[:, 0].add(a[:, 0] * h0)
    _, h = lax.associative_scan(_lin_combine, (a, b), axis=1)
    return h, h[:, -1]


def gated_delta(q, k, v, g, beta, S0):
    B, L, H, Dk = q.shape
    Dv = v.shape[-1]
    C = min(CHUNK, L)
    n = -(-L // C)
    pad = n * C - L

    def blocks(t):
        t = jnp.pad(t, [(0, 0), (0, pad)] + [(0, 0)] * (t.ndim - 2))
        t = t.reshape((B, n, C) + t.shape[2:])
        return jnp.moveaxis(t, 3, 2).swapaxes(0, 1)

    qb, kb, vb, gb, bb = blocks(q), blocks(k), blocks(v), blocks(g), blocks(beta)
    G = jnp.cumsum(gb, axis=-1)
    idx = jnp.arange(C)
    incl = idx[:, None] >= idx[None, :]
    strict = idx[:, None] > idx[None, :]
    decay = jnp.exp(jnp.where(incl, G[..., :, None] - G[..., None, :], -jnp.inf))
    kbeta = kb * bb[..., None]
    A = jnp.where(strict, jnp.einsum('nbhid,nbhjd->nbhij', kbeta, kb) * decay, 0.0)
    IA = A + jnp.eye(C, dtype=A.dtype)
    rhs = jnp.concatenate([vb * bb[..., None], kbeta * jnp.exp(G)[..., None]], axis=-1)
    sol = lax.linalg.triangular_solve(IA, rhs, left_side=True, lower=True, unit_diagonal=True)
    u, w = sol[..., :Dv], sol[..., Dv:]
    att = jnp.einsum('nbhid,nbhjd->nbhij', qb, kb) * decay

    def step(S, inp):
        qc, kc, uc, wc, ac, Gc = inp
        v_new = uc - jnp.einsum('bhcd,bhde->bhce', wc, S)
        o = jnp.einsum('bhcd,bhde->bhce', qc * jnp.exp(Gc)[..., None], S) + jnp.einsum('bhij,bhje->bhie', ac, v_new)
        gl = Gc[..., -1]
        S = S * jnp.exp(gl)[..., None, None] + jnp.einsum('bhcd,bhce->bhde', kc * jnp.exp(gl[..., None] - Gc)[..., None], v_new)
        return S, o

    S, o = lax.scan(step, S0, (qb, kb, u, w, att, G))
    o = jnp.moveaxis(o.swapaxes(0, 1), 2, 3).reshape(B, n * C, H, Dv)[:, :L]
    return o, S


def mixer(h, conv_buf, rg_h0, S0, w_in, conv_w, conv_b_rg, rg_w_a, rg_b_a, rg_w_x, rg_b_x,
          rg_lambda, dn_a_log, dn_dt_bias, dn_norm_w, w_out):
    B, L, _ = h.shape
    f32 = jnp.float32
    proj = h @ w_in
    conv_out, new_buf = causal_conv(proj[..., :CONV_CH], conv_buf, conv_w)
    conv_out = conv_out.astype(f32)
    gates = proj[..., CONV_CH:CONV_CH + RG_WIDTH + DN_WIDTH].astype(f32)
    scal = proj[..., CONV_CH + RG_WIDTH + DN_WIDTH:].astype(f32)
    xr = conv_out[..., :RG_WIDTH] + conv_b_rg.astype(f32)
    rg_h, rg_last = rg_lru(xr, rg_h0.astype(f32), rg_w_a, rg_b_a, rg_w_x, rg_b_x, rg_lambda)
    rg_out = rg_h * jax.nn.gelu(gates[..., :RG_WIDTH])
    qkv = jax.nn.silu(conv_out[..., RG_WIDTH:]).reshape(B, L, 3, DN_HEADS, DN_HEAD_DIM)
    q = l2norm(qkv[:, :, 0]) * (DN_HEAD_DIM ** -0.5)
    k = l2norm(qkv[:, :, 1])
    v = qkv[:, :, 2]
    beta = jax.nn.sigmoid(scal[..., :DN_HEADS])
    g = -jnp.exp(dn_a_log.astype(f32)) * jax.nn.softplus(scal[..., DN_HEADS:] + dn_dt_bias.astype(f32))
    o, S = gated_delta(q, k, v, g, beta, S0.astype(f32))
    o = rmsnorm(o, dn_norm_w) * jax.nn.silu(gates[..., RG_WIDTH:].reshape(B, L, DN_HEADS, DN_HEAD_DIM))
    mixed = jnp.concatenate([rg_out, o.reshape(B, L, DN_WIDTH)], axis=-1).astype(h.dtype)
    return mixed @ w_out, new_buf, rg_last, S


def setup_inputs(seed: int = 0) -> dict:
    key = jax.random.key(seed)
    ks = iter(jax.random.split(key, 40))

    def nrm(shape, scale):
        return jax.random.normal(next(ks), shape, jnp.float32) * scale

    def gain(shape):
        return 1.0 + nrm(shape, 0.05)

    a0 = jax.random.uniform(next(ks), (DEPTH, RG_WIDTH), jnp.float32, 0.9, 0.999)
    p = a0 ** (1.0 / RG_C)
    rg_lambda = jnp.log(p) - jnp.log1p(-p)
    dn_a_log = jnp.log(jax.random.uniform(next(ks), (DEPTH, DN_HEADS), jnp.float32, 1.0, 16.0))
    dt = jnp.exp(jax.random.uniform(next(ks), (DEPTH, DN_HEADS), jnp.float32, np.log(1e-3), np.log(1e-1)))
    dn_dt_bias = jnp.log(jnp.expm1(dt))
    return {
        'x_prompt': nrm((BATCH, SEQ, D_MODEL), 1.0),
        'x_sample': nrm((DEC_BATCH, DEC_SEQ, D_MODEL), 1.0),
        'state_conv': nrm((DEPTH, DEC_BATCH, CONV_WIDTH - 1, CONV_CH), 1.0),
        'state_rglru': nrm((DEPTH, DEC_BATCH, RG_WIDTH), 0.5),
        'state_delta': nrm((DEPTH, DEC_BATCH, DN_HEADS, DN_HEAD_DIM, DN_HEAD_DIM), 0.1),
        'ffn1_norm_pre': gain((DEPTH, D_MODEL)),
        'ffn1_w_up': nrm((DEPTH, D_MODEL, 2 * D_FF), D_MODEL ** -0.5),
        'ffn1_w_down': nrm((DEPTH, D_FF, D_MODEL), D_FF ** -0.5),
        'ffn1_norm_post': gain((DEPTH, D_MODEL)),
        'mix_norm_pre': gain((DEPTH, D_MODEL)),
        'w_in': nrm((DEPTH, D_MODEL, IN_COLS), D_MODEL ** -0.5),
        'conv_w': nrm((DEPTH, CONV_WIDTH, CONV_CH), CONV_WIDTH ** -0.5),
        'conv_b_rg': nrm((DEPTH, RG_WIDTH), 0.02),
        'rg_w_a': nrm((DEPTH, RG_BLOCKS, RG_BLOCK, RG_BLOCK), RG_BLOCK ** -0.5),
        'rg_b_a': nrm((DEPTH, RG_WIDTH), 0.02),
        'rg_w_x': nrm((DEPTH, RG_BLOCKS, RG_BLOCK, RG_BLOCK), RG_BLOCK ** -0.5),
        'rg_b_x': nrm((DEPTH, RG_WIDTH), 0.02),
        'rg_lambda': rg_lambda,
        'dn_a_log': dn_a_log,
        'dn_dt_bias': dn_dt_bias,
        'dn_norm_w': gain((DEPTH, DN_HEAD_DIM)),
        'w_out': nrm((DEPTH, D_MIX, D_MODEL), D_MIX ** -0.5),
        'mix_norm_post': gain((DEPTH, D_MODEL)),
        'ffn2_norm_pre': gain((DEPTH, D_MODEL)),
        'ffn2_w_up': nrm((DEPTH, D_MODEL, 2 * D_FF), D_MODEL ** -0.5),
        'ffn2_w_down': nrm((DEPTH, D_FF, D_MODEL), D_FF ** -0.5),
        'ffn2_norm_post': gain((DEPTH, D_MODEL)),
        'final_norm': gain((D_MODEL,)),
    }


def reference(x_prompt, x_sample, state_conv, state_rglru, state_delta,
              ffn1_norm_pre, ffn1_w_up, ffn1_w_down, ffn1_norm_post,
              mix_norm_pre, w_in, conv_w, conv_b_rg, rg_w_a, rg_b_a, rg_w_x, rg_b_x,
              rg_lambda, dn_a_log, dn_dt_bias, dn_norm_w, w_out, mix_norm_post,
              ffn2_norm_pre, ffn2_w_up, ffn2_w_down, ffn2_norm_post, final_norm):

    def run(x, conv_st, rg_st, dn_st):
        new_conv, new_rg, new_dn = [], [], []
        for l in range(DEPTH):
            h = swiglu(rmsnorm(x, ffn1_norm_pre[l]), ffn1_w_up[l], ffn1_w_down[l])
            x = x + 0.5 * rmsnorm(h, ffn1_norm_post[l])
            h, cb, rh, S = mixer(rmsnorm(x, mix_norm_pre[l]), conv_st[l], rg_st[l], dn_st[l],
                                 w_in[l], conv_w[l], conv_b_rg[l], rg_w_a[l], rg_b_a[l],
                                 rg_w_x[l], rg_b_x[l], rg_lambda[l], dn_a_log[l],
                                 dn_dt_bias[l], dn_norm_w[l], w_out[l])
            x = x + rmsnorm(h, mix_norm_post[l])
            h = swiglu(rmsnorm(x, ffn2_norm_pre[l]), ffn2_w_up[l], ffn2_w_down[l])
            x = x + 0.5 * rmsnorm(h, ffn2_norm_post[l])
            new_conv.append(cb.astype(conv_st.dtype))
            new_rg.append(rh.astype(rg_st.dtype))
            new_dn.append(S.astype(dn_st.dtype))
        y = rmsnorm(x, final_norm)
        return y, jnp.stack(new_conv), jnp.stack(new_rg), jnp.stack(new_dn)

    dt = x_prompt.dtype
    zero_conv = jnp.zeros((DEPTH, BATCH, CONV_WIDTH - 1, CONV_CH), dt)
    zero_rg = jnp.zeros((DEPTH, BATCH, RG_WIDTH), dt)
    zero_dn = jnp.zeros((DEPTH, BATCH, DN_HEADS, DN_HEAD_DIM, DN_HEAD_DIM), dt)
    y_prompt, conv_p, rg_p, dn_p = run(x_prompt, zero_conv, zero_rg, zero_dn)
    y_sample, conv_s, rg_s, dn_s = run(x_sample, state_conv, state_rglru, state_delta)
    return (y_prompt, y_sample, conv_p, rg_p, dn_p, conv_s, rg_s, dn_s)
```

```python
import functools

import jax
import jax.numpy as jnp
from jax import lax
from jax.experimental import pallas as pl
from jax.experimental.pallas import tpu as pltpu

F32 = jnp.float32
BF16 = jnp.bfloat16

D_MODEL = 1024
D_FF = 2816
RG_WIDTH = 512
RG_BLOCKS = 8
RG_BLOCK = RG_WIDTH // RG_BLOCKS
RG_C = 8.0
DN_HEADS = 4
DN_HEAD_DIM = 128
DN_WIDTH = DN_HEADS * DN_HEAD_DIM
CONV_WIDTH = 4
CONV_CH = RG_WIDTH + 3 * DN_WIDTH
GATE_COLS = RG_WIDTH + DN_WIDTH
EPS = 1e-6

LANES = 128
SUBLANES = 8
CHUNK = 64
CONV_PAD = SUBLANES
VMEM_LIMIT = 56 * 1024 * 1024


def _rms(x, g):
    return x * lax.rsqrt(jnp.mean(x * x, axis=-1, keepdims=True) + EPS) * g


def _silu(x):
    return x * jax.nn.sigmoid(x)


def _softplus(x):
    return jnp.maximum(x, 0.0) + jnp.log1p(jnp.exp(-jnp.abs(x)))


def _bdot(a, b):
    return jnp.dot(a.astype(BF16), b.astype(BF16), preferred_element_type=F32)


def _bdot_nt(a, b):
    return lax.dot_general(a.astype(BF16), b.astype(BF16), (((1,), (1,)), ((), ())),
                           preferred_element_type=F32)


def _fdot(a, b):
    return jnp.dot(a, b, preferred_element_type=F32, precision=lax.Precision.HIGHEST)


def _fdot_nt(a, b):
    return lax.dot_general(a, b, (((1,), (1,)), ((), ())), preferred_element_type=F32,
                           precision=lax.Precision.HIGHEST)


def _const_spec(shape):
    nd = len(shape)
    return pl.BlockSpec(shape, lambda *_: (0,) * nd)


def _ffn_kernel(x_ref, gpre_ref, wup_ref, wdn_ref, gpost_ref, *rest, tf, final):
    if final:
        gfin_ref, o_ref, a_ref = rest
    else:
        o_ref, a_ref = rest
    x = x_ref[...]
    h = _rms(x, gpre_ref[...]).astype(BF16)
    for j in range(D_FF // tf):
        g = jnp.dot(h, wup_ref[:, j * tf:(j + 1) * tf], preferred_element_type=F32)
        u = jnp.dot(h, wup_ref[:, D_FF + j * tf:D_FF + (j + 1) * tf], preferred_element_type=F32)
        a_ref[:, j * tf:(j + 1) * tf] = (_silu(g) * u).astype(BF16)
    y = jnp.dot(a_ref[...], wdn_ref[...], preferred_element_type=F32)
    out = x + 0.5 * _rms(y, gpost_ref[...])
    if final:
        out = _rms(out, gfin_ref[...])
    o_ref[...] = out


def _ffn(x2d, gpre, wup, wdn, gpost, gfin=None, *, tm=512, tf=256):
    n = x2d.shape[0]
    final = gfin is not None
    in_specs = [
        pl.BlockSpec((tm, D_MODEL), lambda i: (i, 0)),
        _const_spec((1, D_MODEL)),
        _const_spec((D_MODEL, 2 * D_FF)),
        _const_spec((D_FF, D_MODEL)),
        _const_spec((1, D_MODEL)),
    ]
    args = [x2d, gpre, wup, wdn, gpost]
    if final:
        in_specs.append(_const_spec((1, D_MODEL)))
        args.append(gfin)
    return pl.pallas_call(
        functools.partial(_ffn_kernel, tf=tf, final=final),
        grid=(n // tm,),
        in_specs=in_specs,
        out_specs=pl.BlockSpec((tm, D_MODEL), lambda i: (i, 0)),
        out_shape=jax.ShapeDtypeStruct((n, D_MODEL), F32),
        scratch_shapes=[pltpu.VMEM((tm, D_FF), BF16)],
        compiler_params=pltpu.CompilerParams(
            dimension_semantics=("arbitrary",), vmem_limit_bytes=VMEM_LIMIT),
        name="ffn_final" if final else "ffn",
    )(*args)


def _prep_kernel(x_ref, gpre_ref, wmain_ref, wscal_ref, cw_ref, cb_ref, cst_ref, wrg_ref, brg_ref,
                 lam_ref, alog_ref, dtb_ref,
                 a_ref, b_ref, rgate_ref, q_ref, k_ref, v_ref, zg_ref, scal_ref, ncst_ref,
                 cscr_ref, *, ns, t):
    r = ns * t
    ti = pl.program_id(1)
    x = x_ref[...].reshape(r, D_MODEL)
    h = _rms(x, gpre_ref[...]).astype(BF16)
    proj = jnp.dot(h, wmain_ref[...], preferred_element_type=F32)
    scal = jnp.dot(h, wscal_ref[...], preferred_element_type=F32)

    @pl.when(ti == 0)
    def _():
        cscr_ref[:, CONV_PAD - (CONV_WIDTH - 1):CONV_PAD, :] = cst_ref[...]

    cscr_ref[:, CONV_PAD:CONV_PAD + t, :] = proj[:, :CONV_CH].reshape(ns, t, CONV_CH)
    conv = None
    for j in range(CONV_WIDTH):
        lo = CONV_PAD - (CONV_WIDTH - 1) + j
        term = cscr_ref[:, lo:lo + t, :] * cw_ref[j:j + 1, :]
        conv = term if conv is None else conv + term
    hist = cscr_ref[:, CONV_PAD + t - (CONV_WIDTH - 1):CONV_PAD + t, :]
    ncst_ref[...] = hist
    cscr_ref[:, CONV_PAD - (CONV_WIDTH - 1):CONV_PAD, :] = hist
    conv = conv.reshape(r, CONV_CH)

    xr = conv[:, :RG_WIDTH] + cb_ref[...]
    gax = jnp.dot(xr.astype(BF16), wrg_ref[...], preferred_element_type=F32) + brg_ref[...]
    rr = jax.nn.sigmoid(gax[:, :RG_WIDTH])
    ii = jax.nn.sigmoid(gax[:, RG_WIDTH:])
    log_a = -RG_C * rr * _softplus(-lam_ref[...])
    a = jnp.exp(log_a)
    a_ref[...] = a.reshape(ns, t, RG_WIDTH)
    one_minus_a2 = -jnp.tanh(log_a) * (a * a + 1.0)
    b_ref[...] = (jnp.sqrt(one_minus_a2) * (ii * xr)).reshape(ns, t, RG_WIDTH)
    rgate_ref[...] = jax.nn.gelu(proj[:, CONV_CH:CONV_CH + RG_WIDTH]).reshape(ns, t, RG_WIDTH)

    qkv = _silu(conv[:, RG_WIDTH:])
    for hh in range(DN_HEADS):
        sl = slice(hh * DN_HEAD_DIM, (hh + 1) * DN_HEAD_DIM)
        qh = qkv[:, sl]
        kh = qkv[:, DN_WIDTH + hh * DN_HEAD_DIM:DN_WIDTH + (hh + 1) * DN_HEAD_DIM]
        qn = qh * lax.rsqrt(jnp.sum(qh * qh, axis=-1, keepdims=True) + EPS) * (DN_HEAD_DIM ** -0.5)
        kn = kh * lax.rsqrt(jnp.sum(kh * kh, axis=-1, keepdims=True) + EPS)
        q_ref[:, :, sl] = qn.reshape(ns, t, DN_HEAD_DIM)
        k_ref[:, :, sl] = kn.reshape(ns, t, DN_HEAD_DIM)
    v_ref[...] = qkv[:, 2 * DN_WIDTH:].reshape(ns, t, DN_WIDTH)
    zg_ref[...] = _silu(proj[:, CONV_CH + RG_WIDTH:]).reshape(ns, t, DN_WIDTH)

    lane = lax.broadcasted_iota(jnp.int32, scal.shape, 1)
    beta = jax.nn.sigmoid(scal)
    g = -jnp.exp(alog_ref[...]) * _softplus(scal + dtb_ref[...])
    scal_ref[...] = jnp.where(lane < DN_HEADS, beta, g).reshape(ns, t, LANES)


def _prep(x3d, conv_state, gpre, wmain, wscal, cw, cb, wrg, brg, lam, alog, dtb, *, ns, t):
    bn, l, _ = x3d.shape
    grid = (bn // ns, l // t)
    row = lambda c: pl.BlockSpec((ns, t, c), lambda b, i: (b, i, 0))
    outs = [RG_WIDTH, RG_WIDTH, RG_WIDTH, DN_WIDTH, DN_WIDTH, DN_WIDTH, DN_WIDTH, LANES]
    st_spec = pl.BlockSpec((ns, CONV_WIDTH - 1, CONV_CH), lambda b, i: (b, 0, 0))
    return pl.pallas_call(
        functools.partial(_prep_kernel, ns=ns, t=t),
        grid=grid,
        in_specs=[
            row(D_MODEL),
            _const_spec((1, D_MODEL)),
            _const_spec(wmain.shape),
            _const_spec(wscal.shape),
            _const_spec(cw.shape),
            _const_spec(cb.shape),
            st_spec,
            _const_spec(wrg.shape),
            _const_spec(brg.shape),
            _const_spec(lam.shape),
            _const_spec(alog.shape),
            _const_spec(dtb.shape),
        ],
        out_specs=[row(c) for c in outs] + [st_spec],
        out_shape=[jax.ShapeDtypeStruct((bn, l, c), F32) for c in outs]
        + [jax.ShapeDtypeStruct((bn, CONV_WIDTH - 1, CONV_CH), F32)],
        scratch_shapes=[pltpu.VMEM((ns, t + CONV_PAD, CONV_CH), F32)],
        compiler_params=pltpu.CompilerParams(
            dimension_semantics=("arbitrary", "arbitrary"), vmem_limit_bytes=VMEM_LIMIT),
        name="mixer_prep",
    )(x3d, gpre, wmain, wscal, cw, cb, conv_state, wrg, brg, lam, alog, dtb)


def _tri_inverse(a_strict, seg):
    n = a_strict.shape[0]
    eye = (lax.broadcasted_iota(jnp.int32, (n, n), 0)
           == lax.broadcasted_iota(jnp.int32, (n, n), 1)).astype(F32)
    p = eye - a_strict
    xk = a_strict
    span = 2
    while span < seg:
        xk = _bdot(xk, xk)
        p = p + _bdot(p, xk)
        span *= 2
    return p


def _recur_kernel(a_ref, b_ref, rgate_ref, q_ref, k_ref, v_ref, zg_ref, scal_ref, h0_ref, s0_ref, nw_ref,
                  mix_ref, hout_ref, sout_ref,
                  ascr_ref, bscr_ref, *, ns, t):
    r = ns * t
    seg = min(CHUNK, t)
    nseg = CHUNK // seg
    ti = pl.program_id(1)

    @pl.when(ti == 0)
    def _():
        hout_ref[...] = h0_ref[...]
        sout_ref[...] = s0_ref[...]

    a = a_ref[...].reshape(r, RG_WIDTH)
    b = b_ref[...].reshape(r, RG_WIDTH)
    rowmod = lax.broadcasted_iota(jnp.int32, (r, RG_WIDTH), 0) & (SUBLANES - 1)
    for s in (1, 2, 4):
        m = rowmod >= s
        a_sh = pltpu.roll(a, s, axis=0)
        b_sh = pltpu.roll(b, s, axis=0)
        b = jnp.where(m, a * b_sh + b, b)
        a = jnp.where(m, a * a_sh, a)
    ascr_ref[...] = a.reshape(ns, t, RG_WIDTH)
    bscr_ref[...] = b.reshape(ns, t, RG_WIDTH)

    def rg_body(gi, hc):
        rows = pl.ds(pl.multiple_of(gi * SUBLANES, SUBLANES), SUBLANES)
        hb = ascr_ref[:, rows, :] * hc + bscr_ref[:, rows, :]
        mix_ref[:, rows, :RG_WIDTH] = (hb * rgate_ref[:, rows, :]).astype(mix_ref.dtype)
        return hb[:, SUBLANES - 1:SUBLANES, :]

    hout_ref[...] = lax.fori_loop(0, t // SUBLANES, rg_body, hout_ref[...])

    ri = lax.broadcasted_iota(jnp.int32, (CHUNK, CHUNK), 0)
    ci = lax.broadcasted_iota(jnp.int32, (CHUNK, CHUNK), 1)
    seg_shift = seg.bit_length() - 1
    same = (ri >> seg_shift) == (ci >> seg_shift)
    incl = jnp.logical_and(same, ri >= ci)
    strict = jnp.logical_and(same, ri > ci)
    incl_f = incl.astype(F32)
    sel = (lax.broadcasted_iota(jnp.int32, (SUBLANES, LANES), 0)
           == lax.broadcasted_iota(jnp.int32, (SUBLANES, LANES), 1)).astype(F32)
    nw = nw_ref[...]

    def chunk_index(c):
        if t >= CHUNK:
            per_seq = t // CHUNK
            return c // per_seq, pl.ds(pl.multiple_of((c % per_seq) * CHUNK, CHUNK), CHUNK)
        per_chunk = CHUNK // t
        return pl.ds(pl.multiple_of(c * per_chunk, per_chunk), per_chunk), slice(None)

    def load_rows(ref, c, lanes):
        i0, i1 = chunk_index(c)
        return ref[i0, i1, lanes].reshape(CHUNK, -1)

    def store_rows(ref, c, lanes, val):
        i0, i1 = chunk_index(c)
        if t >= CHUNK:
            ref[i0, i1, lanes] = val
        else:
            ref[i0, i1, lanes] = val.reshape(CHUNK // t, t, -1)

    def chunk_body(c, carry):
        sc = load_rows(scal_ref, c, slice(None))
        gall = _fdot(incl_f, sc)
        gt = _fdot_nt(sel, gall)
        for hh in range(DN_HEADS):
            sl = slice(hh * DN_HEAD_DIM, (hh + 1) * DN_HEAD_DIM)
            qh = load_rows(q_ref, c, sl)
            kh = load_rows(k_ref, c, sl)
            vh = load_rows(v_ref, c, sl)
            beta = sc[:, hh:hh + 1]
            gc = gall[:, DN_HEADS + hh:DN_HEADS + hh + 1]
            gr = gt[DN_HEADS + hh:DN_HEADS + hh + 1, :]
            dec = jnp.where(incl, jnp.exp(jnp.where(incl, gc - gr, 0.0)), 0.0)
            kb = kh * beta
            amat = jnp.where(strict, _bdot_nt(kb, kh) * dec, 0.0)
            att = _bdot_nt(qh, kh) * dec
            tinv = _tri_inverse(amat, seg)
            eg = jnp.exp(gc)
            sol = _bdot(tinv, jnp.concatenate([vh * beta, kb * eg], axis=1))
            u = sol[:, :DN_HEAD_DIM]
            w = sol[:, DN_HEAD_DIM:]
            qe = qh * eg
            vnews = []
            oints = []
            for jj in range(nseg):
                rs = slice(jj * seg, (jj + 1) * seg)
                sidx = c // (t // CHUNK) if t >= CHUNK else c * nseg + jj
                s_old = sout_ref[sidx, hh]
                wq = jnp.concatenate([w[rs], qe[rs]], axis=0)
                rr = _bdot(wq, s_old)
                vnew = u[rs] - rr[:seg]
                gl = gc[(jj + 1) * seg - 1:(jj + 1) * seg, :]
                kdec = kh[rs] * jnp.exp(gl - gc[rs])
                s_new = s_old * jnp.exp(gl) + _bdot(kdec.T, vnew)
                sout_ref[sidx, hh] = s_new
                vnews.append(vnew)
                oints.append(rr[seg:])
            vnew_all = vnews[0] if nseg == 1 else jnp.concatenate(vnews, axis=0)
            oint_all = oints[0] if nseg == 1 else jnp.concatenate(oints, axis=0)
            o = oint_all + _bdot(att, vnew_all)
            o = _rms(o, nw) * load_rows(zg_ref, c, sl)
            store_rows(mix_ref, c, slice(RG_WIDTH + hh * DN_HEAD_DIM, RG_WIDTH + (hh + 1) * DN_HEAD_DIM),
                       o.astype(mix_ref.dtype))
        return carry

    lax.fori_loop(0, r // CHUNK, chunk_body, 0)


def _recur(a, b, rgate, q, k, v, zg, scal, h0, s0, nw, *, ns, t):
    bn, l, _ = a.shape
    grid = (bn // ns, l // t)
    row = lambda c: pl.BlockSpec((ns, t, c), lambda bi, i: (bi, i, 0))
    h_spec = pl.BlockSpec((ns, 1, RG_WIDTH), lambda bi, i: (bi, 0, 0))
    s_spec = pl.BlockSpec((ns, DN_HEADS, DN_HEAD_DIM, DN_HEAD_DIM), lambda bi, i: (bi, 0, 0, 0))
    return pl.pallas_call(
        functools.partial(_recur_kernel, ns=ns, t=t),
        grid=grid,
        in_specs=[row(RG_WIDTH)] * 3 + [row(DN_WIDTH)] * 4 + [row(LANES), h_spec, s_spec,
                                                                _const_spec((1, DN_HEAD_DIM))],
        out_specs=[row(D_MODEL), h_spec, s_spec],
        out_shape=[
            jax.ShapeDtypeStruct((bn, l, D_MODEL), BF16),
            jax.ShapeDtypeStruct((bn, 1, RG_WIDTH), F32),
            jax.ShapeDtypeStruct((bn, DN_HEADS, DN_HEAD_DIM, DN_HEAD_DIM), F32),
        ],
        scratch_shapes=[pltpu.VMEM((ns, t, RG_WIDTH), F32), pltpu.VMEM((ns, t, RG_WIDTH), F32)],
        compiler_params=pltpu.CompilerParams(
            dimension_semantics=("arbitrary", "arbitrary"), vmem_limit_bytes=VMEM_LIMIT),
        name="mixer_recur",
    )(a, b, rgate, q, k, v, zg, scal, h0, s0, nw)


def _oproj_kernel(x_ref, m_ref, w_ref, g_ref, o_ref):
    y = jnp.dot(m_ref[...], w_ref[...], preferred_element_type=F32)
    o_ref[...] = x_ref[...] + _rms(y, g_ref[...])


def _oproj(x2d, mixed2d, w, g, *, tm=512):
    n = x2d.shape[0]
    return pl.pallas_call(
        _oproj_kernel,
        grid=(n // tm,),
        in_specs=[
            pl.BlockSpec((tm, D_MODEL), lambda i: (i, 0)),
            pl.BlockSpec((tm, D_MODEL), lambda i: (i, 0)),
            _const_spec((D_MODEL, D_MODEL)),
            _const_spec((1, D_MODEL)),
        ],
        out_specs=pl.BlockSpec((tm, D_MODEL), lambda i: (i, 0)),
        out_shape=jax.ShapeDtypeStruct((n, D_MODEL), F32),
        compiler_params=pltpu.CompilerParams(
            dimension_semantics=("arbitrary",), vmem_limit_bytes=VMEM_LIMIT),
        name="mixer_oproj",
    )(x2d, mixed2d, w, g)


def _lane_pad(vec, offset):
    out = jnp.zeros((1, LANES), F32)
    return lax.dynamic_update_slice(out, vec.reshape(1, -1).astype(F32), (0, offset))


def _layer_weights(l, ffn1_norm_pre, ffn1_w_up, ffn1_w_down, ffn1_norm_post, mix_norm_pre, w_in, conv_w,
                   conv_b_rg, rg_w_a, rg_b_a, rg_w_x, rg_b_x, rg_lambda, dn_a_log, dn_dt_bias, dn_norm_w,
                   w_out, mix_norm_post, ffn2_norm_pre, ffn2_w_up, ffn2_w_down, ffn2_norm_post):
    row = lambda v: v[l].reshape(1, -1)
    main_cols = CONV_CH + GATE_COLS
    wscal = jnp.zeros((D_MODEL, LANES), BF16).at[:, :2 * DN_HEADS].set(w_in[l][:, main_cols:].astype(BF16))
    eye = jnp.eye(RG_BLOCKS, dtype=F32)
    bd = lambda w: jnp.einsum('hij,hg->higj', w, eye).reshape(RG_WIDTH, RG_WIDTH)
    wrg = jnp.concatenate([bd(rg_w_a[l]), bd(rg_w_x[l])], axis=1).astype(BF16)
    return dict(
        f1=(row(ffn1_norm_pre), ffn1_w_up[l].astype(BF16), ffn1_w_down[l].astype(BF16), row(ffn1_norm_post)),
        f2=(row(ffn2_norm_pre), ffn2_w_up[l].astype(BF16), ffn2_w_down[l].astype(BF16), row(ffn2_norm_post)),
        prep=(row(mix_norm_pre), w_in[l][:, :main_cols].astype(BF16), wscal, conv_w[l], row(conv_b_rg), wrg,
              jnp.concatenate([row(rg_b_a), row(rg_b_x)], axis=1), row(rg_lambda),
              _lane_pad(dn_a_log[l], DN_HEADS), _lane_pad(dn_dt_bias[l], DN_HEADS)),
        nw=row(dn_norm_w),
        wout=w_out[l].astype(BF16),
        gpost=row(mix_norm_post),
    )


def _run(x, conv_st, rg_st, dn_st, layers, final_norm, *, ns, t):
    bn, l, _ = x.shape
    n = bn * l
    x2 = x.reshape(n, D_MODEL)
    new_conv, new_rg, new_dn = [], [], []
    depth = len(layers)
    for li, lw in enumerate(layers):
        x2 = _ffn(x2, *lw["f1"])
        gpre, wmain, wscal, cw, cb, wrg, brg, lam, alog, dtb = lw["prep"]
        a, b, rgate, q, k, v, zg, scal, ncst = _prep(
            x2.reshape(bn, l, D_MODEL), conv_st[li], gpre, wmain, wscal, cw, cb, wrg, brg, lam, alog, dtb,
            ns=ns, t=t)
        mixed, hlast, s_new = _recur(a, b, rgate, q, k, v, zg, scal,
                                     rg_st[li].reshape(bn, 1, RG_WIDTH), dn_st[li], lw["nw"], ns=ns, t=t)
        x2 = _oproj(x2, mixed.reshape(n, D_MODEL), lw["wout"], lw["gpost"])
        x2 = _ffn(x2, *lw["f2"], gfin=final_norm.reshape(1, -1) if li == depth - 1 else None)
        new_conv.append(ncst)
        new_rg.append(hlast.reshape(bn, RG_WIDTH))
        new_dn.append(s_new)
    return x2.reshape(bn, l, D_MODEL), jnp.stack(new_conv), jnp.stack(new_rg), jnp.stack(new_dn)


def kernel(x_prompt, x_sample, state_conv, state_rglru, state_delta, ffn1_norm_pre, ffn1_w_up, ffn1_w_down, ffn1_norm_post, mix_norm_pre, w_in, conv_w, conv_b_rg, rg_w_a, rg_b_a, rg_w_x, rg_b_x, rg_lambda, dn_a_log, dn_dt_bias, dn_norm_w, w_out, mix_norm_post, ffn2_norm_pre, ffn2_w_up, ffn2_w_down, ffn2_norm_post, final_norm):
    depth = w_in.shape[0]
    layers = [
        _layer_weights(l, ffn1_norm_pre, ffn1_w_up, ffn1_w_down, ffn1_norm_post, mix_norm_pre, w_in, conv_w,
                       conv_b_rg, rg_w_a, rg_b_a, rg_w_x, rg_b_x, rg_lambda, dn_a_log, dn_dt_bias, dn_norm_w,
                       w_out, mix_norm_post, ffn2_norm_pre, ffn2_w_up, ffn2_w_down, ffn2_norm_post)
        for l in range(depth)
    ]
    bp = x_prompt.shape[0]
    dt = x_prompt.dtype
    zero_conv = jnp.zeros((depth, bp, CONV_WIDTH - 1, CONV_CH), dt)
    zero_rg = jnp.zeros((depth, bp, RG_WIDTH), dt)
    zero_dn = jnp.zeros((depth, bp, DN_HEADS, DN_HEAD_DIM, DN_HEAD_DIM), dt)
    y_p, conv_p, rg_p, dn_p = _run(x_prompt, zero_conv, zero_rg, zero_dn, layers, final_norm, ns=1, t=512)
    y_s, conv_s, rg_s, dn_s = _run(x_sample, state_conv, state_rglru, state_delta, layers, final_norm,
                                   ns=16, t=x_sample.shape[1])
    return (y_p, y_s, conv_p, rg_p, dn_p, conv_s, rg_s, dn_s)
```

```python
import functools

import jax
import jax.numpy as jnp
from jax import lax
from jax.experimental import pallas as pl
from jax.experimental.pallas import tpu as pltpu

F32 = jnp.float32
BF16 = jnp.bfloat16

D_MODEL = 1024
D_FF = 2816
RG_WIDTH = 512
RG_BLOCKS = 8
RG_BLOCK = RG_WIDTH // RG_BLOCKS
RG_C = 8.0
DN_HEADS = 4
DN_HEAD_DIM = 128
DN_WIDTH = DN_HEADS * DN_HEAD_DIM
CONV_WIDTH = 4
CONV_CH = RG_WIDTH + 3 * DN_WIDTH
GATE_COLS = RG_WIDTH + DN_WIDTH
EPS = 1e-6

LANES = 128
SUBLANES = 8
CHUNK = 64
CONV_PAD = SUBLANES
VMEM_LIMIT = 56 * 1024 * 1024


def _rms(x, g):
    return x * lax.rsqrt(jnp.mean(x * x, axis=-1, keepdims=True) + EPS) * g


def _silu(x):
    return x * jax.nn.sigmoid(x)


def _softplus(x):
    return jnp.maximum(x, 0.0) + jnp.log1p(jnp.exp(-jnp.abs(x)))


def _bdot(a, b):
    return jnp.dot(a.astype(BF16), b.astype(BF16), preferred_element_type=F32)


def _bdot_nt(a, b):
    return lax.dot_general(a.astype(BF16), b.astype(BF16), (((1,), (1,)), ((), ())),
                           preferred_element_type=F32)


def _fdot(a, b):
    return jnp.dot(a, b, preferred_element_type=F32, precision=lax.Precision.HIGHEST)


def _fdot_nt(a, b):
    return lax.dot_general(a, b, (((1,), (1,)), ((), ())), preferred_element_type=F32,
                           precision=lax.Precision.HIGHEST)


def _const_spec(shape):
    nd = len(shape)
    return pl.BlockSpec(shape, lambda *_: (0,) * nd)


def _ffn_kernel(x_ref, gpre_ref, wup_ref, wdn_ref, gpost_ref, *rest, tf, final):
    if final:
        gfin_ref, o_ref, a_ref = rest
    else:
        o_ref, a_ref = rest
    x = x_ref[...]
    h = _rms(x, gpre_ref[...]).astype(BF16)
    for j in range(D_FF // tf):
        g = jnp.dot(h, wup_ref[:, j * tf:(j + 1) * tf], preferred_element_type=F32)
        u = jnp.dot(h, wup_ref[:, D_FF + j * tf:D_FF + (j + 1) * tf], preferred_element_type=F32)
        a_ref[:, j * tf:(j + 1) * tf] = (_silu(g) * u).astype(BF16)
    y = jnp.dot(a_ref[...], wdn_ref[...], preferred_element_type=F32)
    out = x + 0.5 * _rms(y, gpost_ref[...])
    if final:
        out = _rms(out, gfin_ref[...])
    o_ref[...] = out


def _ffn(x2d, gpre, wup, wdn, gpost, gfin=None, *, tm=512, tf=256):
    n = x2d.shape[0]
    final = gfin is not None
    in_specs = [
        pl.BlockSpec((tm, D_MODEL), lambda i: (i, 0)),
        _const_spec((1, D_MODEL)),
        _const_spec((D_MODEL, 2 * D_FF)),
        _const_spec((D_FF, D_MODEL)),
        _const_spec((1, D_MODEL)),
    ]
    args = [x2d, gpre, wup, wdn, gpost]
    if final:
        in_specs.append(_const_spec((1, D_MODEL)))
        args.append(gfin)
    return pl.pallas_call(
        functools.partial(_ffn_kernel, tf=tf, final=final),
        grid=(n // tm,),
        in_specs=in_specs,
        out_specs=pl.BlockSpec((tm, D_MODEL), lambda i: (i, 0)),
        out_shape=jax.ShapeDtypeStruct((n, D_MODEL), F32),
        scratch_shapes=[pltpu.VMEM((tm, D_FF), BF16)],
        compiler_params=pltpu.CompilerParams(
            dimension_semantics=("arbitrary",), vmem_limit_bytes=VMEM_LIMIT),
        name="ffn_final" if final else "ffn",
    )(*args)


def _prep_kernel(x_ref, gpre_ref, wmain_ref, wscal_ref, cw_ref, cb_ref, cst_ref, wrg_ref, brg_ref,
                 lam_ref, alog_ref, dtb_ref,
                 a_ref, b_ref, rgate_ref, q_ref, k_ref, v_ref, zg_ref, scal_ref, ncst_ref,
                 cscr_ref, *, ns, t):
    r = ns * t
    ti = pl.program_id(1)
    x = x_ref[...].reshape(r, D_MODEL)
    h = _rms(x, gpre_ref[...]).astype(BF16)
    proj = jnp.dot(h, wmain_ref[...], preferred_element_type=F32)
    scal = jnp.dot(h, wscal_ref[...], preferred_element_type=F32)

    @pl.when(ti == 0)
    def _():
        cscr_ref[:, CONV_PAD - (CONV_WIDTH - 1):CONV_PAD, :] = cst_ref[...]

    cscr_ref[:, CONV_PAD:CONV_PAD + t, :] = proj[:, :CONV_CH].reshape(ns, t, CONV_CH)
    conv = None
    for j in range(CONV_WIDTH):
        lo = CONV_PAD - (CONV_WIDTH - 1) + j
        term = cscr_ref[:, lo:lo + t, :] * cw_ref[j:j + 1, :]
        conv = term if conv is None else conv + term
    hist = cscr_ref[:, CONV_PAD + t - (CONV_WIDTH - 1):CONV_PAD + t, :]
    ncst_ref[...] = hist
    cscr_ref[:, CONV_PAD - (CONV_WIDTH - 1):CONV_PAD, :] = hist
    conv = conv.reshape(r, CONV_CH)

    xr = conv[:, :RG_WIDTH] + cb_ref[...]
    gax = jnp.dot(xr.astype(BF16), wrg_ref[...], preferred_element_type=F32) + brg_ref[...]
    rr = jax.nn.sigmoid(gax[:, :RG_WIDTH])
    ii = jax.nn.sigmoid(gax[:, RG_WIDTH:])
    log_a = -RG_C * rr * _softplus(-lam_ref[...])
    a = jnp.exp(log_a)
    a_ref[...] = a.reshape(ns, t, RG_WIDTH)
    one_minus_a2 = -jnp.tanh(log_a) * (a * a + 1.0)
    b_ref[...] = (jnp.sqrt(one_minus_a2) * (ii * xr)).reshape(ns, t, RG_WIDTH)
    rgate_ref[...] = jax.nn.gelu(proj[:, CONV_CH:CONV_CH + RG_WIDTH]).reshape(ns, t, RG_WIDTH)

    qkv = _silu(conv[:, RG_WIDTH:])
    for hh in range(DN_HEADS):
        sl = slice(hh * DN_HEAD_DIM, (hh + 1) * DN_HEAD_DIM)
        qh = qkv[:, sl]
        kh = qkv[:, DN_WIDTH + hh * DN_HEAD_DIM:DN_WIDTH + (hh + 1) * DN_HEAD_DIM]
        qn = qh * lax.rsqrt(jnp.sum(qh * qh, axis=-1, keepdims=True) + EPS) * (DN_HEAD_DIM ** -0.5)
        kn = kh * lax.rsqrt(jnp.sum(kh * kh, axis=-1, keepdims=True) + EPS)
        q_ref[:, :, sl] = qn.reshape(ns, t, DN_HEAD_DIM)
        k_ref[:, :, sl] = kn.reshape(ns, t, DN_HEAD_DIM)
    v_ref[...] = qkv[:, 2 * DN_WIDTH:].reshape(ns, t, DN_WIDTH)
    zg_ref[...] = _silu(proj[:, CONV_CH + RG_WIDTH:]).reshape(ns, t, DN_WIDTH)

    lane = lax.broadcasted_iota(jnp.int32, scal.shape, 1)
    beta = jax.nn.sigmoid(scal)
    g = -jnp.exp(alog_ref[...]) * _softplus(scal + dtb_ref[...])
    scal_ref[...] = jnp.where(lane < DN_HEADS, beta, g).reshape(ns, t, LANES)


def _prep(x3d, conv_state, gpre, wmain, wscal, cw, cb, wrg, brg, lam, alog, dtb, *, ns, t):
    bn, l, _ = x3d.shape
    grid = (bn // ns, l // t)
    row = lambda c: pl.BlockSpec((ns, t, c), lambda b, i: (b, i, 0))
    outs = [RG_WIDTH, RG_WIDTH, RG_WIDTH, DN_WIDTH, DN_WIDTH, DN_WIDTH, DN_WIDTH, LANES]
    st_spec = pl.BlockSpec((ns, CONV_WIDTH - 1, CONV_CH), lambda b, i: (b, 0, 0))
    return pl.pallas_call(
        functools.partial(_prep_kernel, ns=ns, t=t),
        grid=grid,
        in_specs=[
            row(D_MODEL),
            _const_spec((1, D_MODEL)),
            _const_spec(wmain.shape),
            _const_spec(wscal.shape),
            _const_spec(cw.shape),
            _const_spec(cb.shape),
            st_spec,
            _const_spec(wrg.shape),
            _const_spec(brg.shape),
            _const_spec(lam.shape),
            _const_spec(alog.shape),
            _const_spec(dtb.shape),
        ],
        out_specs=[row(c) for c in outs] + [st_spec],
        out_shape=[jax.ShapeDtypeStruct((bn, l, c), F32) for c in outs]
        + [jax.ShapeDtypeStruct((bn, CONV_WIDTH - 1, CONV_CH), F32)],
        scratch_shapes=[pltpu.VMEM((ns, t + CONV_PAD, CONV_CH), F32)],
        compiler_params=pltpu.CompilerParams(
            dimension_semantics=("arbitrary", "arbitrary"), vmem_limit_bytes=VMEM_LIMIT),
        name="mixer_prep",
    )(x3d, gpre, wmain, wscal, cw, cb, conv_state, wrg, brg, lam, alog, dtb)


def _recur_kernel(a_ref, b_ref, rgate_ref, q_ref, k_ref, v_ref, zg_ref, scal_ref, h0_ref, s0_ref, nw_ref,
                  mix_ref, hout_ref, sout_ref,
                  ascr_ref, bscr_ref, *, ns, t, cpi):
    r = ns * t
    seg = min(CHUNK, t)
    nseg = CHUNK // seg
    ti = pl.program_id(1)

    @pl.when(ti == 0)
    def _():
        hout_ref[...] = h0_ref[...]
        sout_ref[...] = s0_ref[...]

    a = a_ref[...].reshape(r, RG_WIDTH)
    b = b_ref[...].reshape(r, RG_WIDTH)
    rowmod = lax.broadcasted_iota(jnp.int32, (r, RG_WIDTH), 0) & (SUBLANES - 1)
    for s in (1, 2, 4):
        m = rowmod >= s
        a_sh = pltpu.roll(a, s, axis=0)
        b_sh = pltpu.roll(b, s, axis=0)
        b = jnp.where(m, a * b_sh + b, b)
        a = jnp.where(m, a * a_sh, a)
    ascr_ref[...] = a.reshape(ns, t, RG_WIDTH)
    bscr_ref[...] = b.reshape(ns, t, RG_WIDTH)

    def rg_body(gi, hc):
        rows = pl.ds(pl.multiple_of(gi * SUBLANES, SUBLANES), SUBLANES)
        hb = ascr_ref[:, rows, :] * hc + bscr_ref[:, rows, :]
        mix_ref[:, rows, :RG_WIDTH] = (hb * rgate_ref[:, rows, :]).astype(mix_ref.dtype)
        return hb[:, SUBLANES - 1:SUBLANES, :]

    hout_ref[...] = lax.fori_loop(0, t // SUBLANES, rg_body, hout_ref[...])

    ri = lax.broadcasted_iota(jnp.int32, (CHUNK, CHUNK), 0)
    ci = lax.broadcasted_iota(jnp.int32, (CHUNK, CHUNK), 1)
    seg_shift = seg.bit_length() - 1
    same = (ri >> seg_shift) == (ci >> seg_shift)
    incl = jnp.logical_and(same, ri >= ci)
    strict = jnp.logical_and(same, ri > ci)
    incl_f = incl.astype(F32)
    sel = (lax.broadcasted_iota(jnp.int32, (SUBLANES, LANES), 0)
           == lax.broadcasted_iota(jnp.int32, (SUBLANES, LANES), 1)).astype(F32)
    nw = nw_ref[...]

    def chunk_index(c):
        if t >= CHUNK:
            per_seq = t // CHUNK
            return c // per_seq, pl.ds(pl.multiple_of((c % per_seq) * CHUNK, CHUNK), CHUNK)
        per_chunk = CHUNK // t
        return pl.ds(pl.multiple_of(c * per_chunk, per_chunk), per_chunk), slice(None)

    def load_rows(ref, c, lanes):
        i0, i1 = chunk_index(c)
        return ref[i0, i1, lanes].reshape(CHUNK, -1)

    def store_rows(ref, c, lanes, val):
        i0, i1 = chunk_index(c)
        if t >= CHUNK:
            ref[i0, i1, lanes] = val
        else:
            ref[i0, i1, lanes] = val.reshape(CHUNK // t, t, -1)

    eye = (ri == ci).astype(F32)

    def state_index(c, jj):
        return c // (t // CHUNK) if t >= CHUNK else c * nseg + jj

    def group_body(gi, carry):
        chains = []
        for cc in range(cpi):
            c = gi * cpi + cc
            sc = load_rows(scal_ref, c, slice(None))
            gall = _fdot(incl_f, sc)
            gt = _fdot_nt(sel, gall)
            for hh in range(DN_HEADS):
                sl = slice(hh * DN_HEAD_DIM, (hh + 1) * DN_HEAD_DIM)
                chains.append(dict(
                    c=c, hh=hh, sl=sl, beta=sc[:, hh:hh + 1],
                    gc=gall[:, DN_HEADS + hh:DN_HEADS + hh + 1],
                    gr=gt[DN_HEADS + hh:DN_HEADS + hh + 1, :]))
        for ch in chains:
            qh = load_rows(q_ref, ch["c"], ch["sl"])
            kh = load_rows(k_ref, ch["c"], ch["sl"])
            dec = jnp.where(incl, jnp.exp(jnp.where(incl, ch["gc"] - ch["gr"], 0.0)), 0.0)
            kb = kh * ch["beta"]
            ch.update(qh=qh, kh=kh, kb=kb,
                      x=jnp.where(strict, _bdot_nt(kb, kh) * dec, 0.0),
                      att=_bdot_nt(qh, kh) * dec)
        for ch in chains:
            ch["p"] = eye - ch["x"]
        span = 2
        while span < seg:
            for ch in chains:
                ch["x"] = _bdot(ch["x"], ch["x"])
            for ch in chains:
                ch["p"] = ch["p"] + _bdot(ch["p"], ch["x"])
            span *= 2
        for ch in chains:
            vh = load_rows(v_ref, ch["c"], ch["sl"])
            eg = jnp.exp(ch["gc"])
            sol = _bdot(ch["p"], jnp.concatenate([vh * ch["beta"], ch["kb"] * eg], axis=1))
            ch.update(u=sol[:, :DN_HEAD_DIM], w=sol[:, DN_HEAD_DIM:], qe=ch["qh"] * eg)
        for ch in chains:
            ch["s_old"], ch["rr"] = [], []
            for jj in range(nseg):
                rs = slice(jj * seg, (jj + 1) * seg)
                s_old = sout_ref[state_index(ch["c"], jj), ch["hh"]]
                ch["s_old"].append(s_old)
                ch["rr"].append(_bdot(jnp.concatenate([ch["w"][rs], ch["qe"][rs]], axis=0), s_old))
        for ch in chains:
            vnews = []
            for jj in range(nseg):
                rs = slice(jj * seg, (jj + 1) * seg)
                vnew = ch["u"][rs] - ch["rr"][jj][:seg]
                gl = ch["gc"][(jj + 1) * seg - 1:(jj + 1) * seg, :]
                kdec = ch["kh"][rs] * jnp.exp(gl - ch["gc"][rs])
                sout_ref[state_index(ch["c"], jj), ch["hh"]] = (
                    ch["s_old"][jj] * jnp.exp(gl) + _bdot(kdec.T, vnew))
                vnews.append(vnew)
            ch["vnew"] = vnews[0] if nseg == 1 else jnp.concatenate(vnews, axis=0)
        for ch in chains:
            oint = ch["rr"][0][seg:] if nseg == 1 else jnp.concatenate([x[seg:] for x in ch["rr"]], axis=0)
            o = oint + _bdot(ch["att"], ch["vnew"])
            o = _rms(o, nw) * load_rows(zg_ref, ch["c"], ch["sl"])
            store_rows(mix_ref, ch["c"], slice(RG_WIDTH + ch["hh"] * DN_HEAD_DIM,
                                               RG_WIDTH + (ch["hh"] + 1) * DN_HEAD_DIM), o.astype(mix_ref.dtype))
        return carry

    lax.fori_loop(0, r // (CHUNK * cpi), group_body, 0)


def _recur(a, b, rgate, q, k, v, zg, scal, h0, s0, nw, *, ns, t, cpi=2):
    cpi = min(8, ns * t // CHUNK)
    bn, l, _ = a.shape
    grid = (bn // ns, l // t)
    row = lambda c: pl.BlockSpec((ns, t, c), lambda bi, i: (bi, i, 0))
    h_spec = pl.BlockSpec((ns, 1, RG_WIDTH), lambda bi, i: (bi, 0, 0))
    s_spec = pl.BlockSpec((ns, DN_HEADS, DN_HEAD_DIM, DN_HEAD_DIM), lambda bi, i: (bi, 0, 0, 0))
    return pl.pallas_call(
        functools.partial(_recur_kernel, ns=ns, t=t, cpi=cpi),
        grid=grid,
        in_specs=[row(RG_WIDTH)] * 3 + [row(DN_WIDTH)] * 4 + [row(LANES), h_spec, s_spec,
                                                                _const_spec((1, DN_HEAD_DIM))],
        out_specs=[row(D_MODEL), h_spec, s_spec],
        out_shape=[
            jax.ShapeDtypeStruct((bn, l, D_MODEL), BF16),
            jax.ShapeDtypeStruct((bn, 1, RG_WIDTH), F32),
            jax.ShapeDtypeStruct((bn, DN_HEADS, DN_HEAD_DIM, DN_HEAD_DIM), F32),
        ],
        scratch_shapes=[pltpu.VMEM((ns, t, RG_WIDTH), F32), pltpu.VMEM((ns, t, RG_WIDTH), F32)],
        compiler_params=pltpu.CompilerParams(
            dimension_semantics=("arbitrary", "arbitrary"), vmem_limit_bytes=VMEM_LIMIT),
        name="mixer_recur",
    )(a, b, rgate, q, k, v, zg, scal, h0, s0, nw)


def _oproj_kernel(x_ref, m_ref, w_ref, g_ref, o_ref):
    y = jnp.dot(m_ref[...], w_ref[...], preferred_element_type=F32)
    o_ref[...] = x_ref[...] + _rms(y, g_ref[...])


def _oproj(x2d, mixed2d, w, g, *, tm=512):
    n = x2d.shape[0]
    return pl.pallas_call(
        _oproj_kernel,
        grid=(n // tm,),
        in_specs=[
            pl.BlockSpec((tm, D_MODEL), lambda i: (i, 0)),
            pl.BlockSpec((tm, D_MODEL), lambda i: (i, 0)),
            _const_spec((D_MODEL, D_MODEL)),
            _const_spec((1, D_MODEL)),
        ],
        out_specs=pl.BlockSpec((tm, D_MODEL), lambda i: (i, 0)),
        out_shape=jax.ShapeDtypeStruct((n, D_MODEL), F32),
        compiler_params=pltpu.CompilerParams(
            dimension_semantics=("arbitrary",), vmem_limit_bytes=VMEM_LIMIT),
        name="mixer_oproj",
    )(x2d, mixed2d, w, g)


def _lane_pad(vec, offset):
    out = jnp.zeros((1, LANES), F32)
    return lax.dynamic_update_slice(out, vec.reshape(1, -1).astype(F32), (0, offset))


def _layer_weights(l, ffn1_norm_pre, ffn1_w_up, ffn1_w_down, ffn1_norm_post, mix_norm_pre, w_in, conv_w,
                   conv_b_rg, rg_w_a, rg_b_a, rg_w_x, rg_b_x, rg_lambda, dn_a_log, dn_dt_bias, dn_norm_w,
                   w_out, mix_norm_post, ffn2_norm_pre, ffn2_w_up, ffn2_w_down, ffn2_norm_post):
    row = lambda v: v[l].reshape(1, -1)
    main_cols = CONV_CH + GATE_COLS
    wscal = jnp.zeros((D_MODEL, LANES), BF16).at[:, :2 * DN_HEADS].set(w_in[l][:, main_cols:].astype(BF16))
    eye = jnp.eye(RG_BLOCKS, dtype=F32)
    bd = lambda w: jnp.einsum('hij,hg->higj', w, eye).reshape(RG_WIDTH, RG_WIDTH)
    wrg = jnp.concatenate([bd(rg_w_a[l]), bd(rg_w_x[l])], axis=1).astype(BF16)
    return dict(
        f1=(row(ffn1_norm_pre), ffn1_w_up[l].astype(BF16), ffn1_w_down[l].astype(BF16), row(ffn1_norm_post)),
        f2=(row(ffn2_norm_pre), ffn2_w_up[l].astype(BF16), ffn2_w_down[l].astype(BF16), row(ffn2_norm_post)),
        prep=(row(mix_norm_pre), w_in[l][:, :main_cols].astype(BF16), wscal, conv_w[l], row(conv_b_rg), wrg,
              jnp.concatenate([row(rg_b_a), row(rg_b_x)], axis=1), row(rg_lambda),
              _lane_pad(dn_a_log[l], DN_HEADS), _lane_pad(dn_dt_bias[l], DN_HEADS)),
        nw=row(dn_norm_w),
        wout=w_out[l].astype(BF16),
        gpost=row(mix_norm_post),
    )


def _run(x, conv_st, rg_st, dn_st, layers, final_norm, *, prep_blk, rec_blk):
    bn, l, _ = x.shape
    n = bn * l
    x2 = x.reshape(n, D_MODEL)
    new_conv, new_rg, new_dn = [], [], []
    depth = len(layers)
    for li, lw in enumerate(layers):
        x2 = _ffn(x2, *lw["f1"])
        gpre, wmain, wscal, cw, cb, wrg, brg, lam, alog, dtb = lw["prep"]
        a, b, rgate, q, k, v, zg, scal, ncst = _prep(
            x2.reshape(bn, l, D_MODEL), conv_st[li], gpre, wmain, wscal, cw, cb, wrg, brg, lam, alog, dtb,
            ns=prep_blk[0], t=prep_blk[1])
        mixed, hlast, s_new = _recur(a, b, rgate, q, k, v, zg, scal,
                                     rg_st[li].reshape(bn, 1, RG_WIDTH), dn_st[li], lw["nw"],
                                     ns=rec_blk[0], t=rec_blk[1])
        x2 = _oproj(x2, mixed.reshape(n, D_MODEL), lw["wout"], lw["gpost"])
        x2 = _ffn(x2, *lw["f2"], gfin=final_norm.reshape(1, -1) if li == depth - 1 else None)
        new_conv.append(ncst)
        new_rg.append(hlast.reshape(bn, RG_WIDTH))
        new_dn.append(s_new)
    return x2.reshape(bn, l, D_MODEL), jnp.stack(new_conv), jnp.stack(new_rg), jnp.stack(new_dn)


def kernel(x_prompt, x_sample, state_conv, state_rglru, state_delta, ffn1_norm_pre, ffn1_w_up, ffn1_w_down, ffn1_norm_post, mix_norm_pre, w_in, conv_w, conv_b_rg, rg_w_a, rg_b_a, rg_w_x, rg_b_x, rg_lambda, dn_a_log, dn_dt_bias, dn_norm_w, w_out, mix_norm_post, ffn2_norm_pre, ffn2_w_up, ffn2_w_down, ffn2_norm_post, final_norm):
    depth = w_in.shape[0]
    layers = [
        _layer_weights(l, ffn1_norm_pre, ffn1_w_up, ffn1_w_down, ffn1_norm_post, mix_norm_pre, w_in, conv_w,
                       conv_b_rg, rg_w_a, rg_b_a, rg_w_x, rg_b_x, rg_lambda, dn_a_log, dn_dt_bias, dn_norm_w,
                       w_out, mix_norm_post, ffn2_norm_pre, ffn2_w_up, ffn2_w_down, ffn2_norm_post)
        for l in range(depth)
    ]
    bp = x_prompt.shape[0]
    dt = x_prompt.dtype
    zero_conv = jnp.zeros((depth, bp, CONV_WIDTH - 1, CONV_CH), dt)
    zero_rg = jnp.zeros((depth, bp, RG_WIDTH), dt)
    zero_dn = jnp.zeros((depth, bp, DN_HEADS, DN_HEAD_DIM, DN_HEAD_DIM), dt)
    y_p, conv_p, rg_p, dn_p = _run(x_prompt, zero_conv, zero_rg, zero_dn, layers, final_norm,
                                   prep_blk=(1, 512), rec_blk=(bp, CHUNK))
    ts = x_sample.shape[1]
    y_s, conv_s, rg_s, dn_s = _run(x_sample, state_conv, state_rglru, state_delta, layers, final_norm,
                                   prep_blk=(64, ts), rec_blk=(16, ts))
    return (y_p, y_s, conv_p, rg_p, dn_p, conv_s, rg_s, dn_s)
```

```python
import functools

import jax
import jax.numpy as jnp
from jax import lax
from jax.experimental import pallas as pl
from jax.experimental.pallas import tpu as pltpu

F32 = jnp.float32
BF16 = jnp.bfloat16

D_MODEL = 1024
D_FF = 2816
RG_WIDTH = 512
RG_BLOCKS = 8
RG_BLOCK = RG_WIDTH // RG_BLOCKS
RG_C = 8.0
DN_HEADS = 4
DN_HEAD_DIM = 128
DN_WIDTH = DN_HEADS * DN_HEAD_DIM
CONV_WIDTH = 4
CONV_CH = RG_WIDTH + 3 * DN_WIDTH
GATE_COLS = RG_WIDTH + DN_WIDTH
MAIN_COLS = CONV_CH + GATE_COLS
EPS = 1e-6

LANES = 128
SUBLANES = 8
CHUNK = 64
CONV_PAD = SUBLANES
TM = 512
FF_TILE = 256
VMEM_LIMIT = 56 * 1024 * 1024


def _rms(x, g):
    return x * lax.rsqrt(jnp.mean(x * x, axis=-1, keepdims=True) + EPS) * g


def _silu(x):
    return x * jax.nn.sigmoid(x)


def _softplus(x):
    return jnp.maximum(x, 0.0) + jnp.log1p(jnp.exp(-jnp.abs(x)))


def _bdot(a, b):
    return jnp.dot(a.astype(BF16), b.astype(BF16), preferred_element_type=F32)


def _bdot_nt(a, b):
    return lax.dot_general(a.astype(BF16), b.astype(BF16), (((1,), (1,)), ((), ())),
                           preferred_element_type=F32)


def _fdot(a, b):
    return jnp.dot(a, b, preferred_element_type=F32, precision=lax.Precision.HIGHEST)


def _fdot_nt(a, b):
    return lax.dot_general(a, b, (((1,), (1,)), ((), ())), preferred_element_type=F32,
                           precision=lax.Precision.HIGHEST)


def _layer_spec(shape, layer, *, single=False):
    nd = len(shape)
    kw = dict(pipeline_mode=pl.Buffered(1)) if single else {}
    return pl.BlockSpec((None,) + tuple(shape[1:]), lambda *_: (layer,) + (0,) * (nd - 1), **kw)


def _cparams(n_grid):
    return pltpu.CompilerParams(dimension_semantics=("arbitrary",) * n_grid, vmem_limit_bytes=VMEM_LIMIT)


def _ffn_kernel(*refs, nbp, two_src, with_mix, final):
    refs = list(refs)
    xp_ref = refs.pop(0)
    xs_ref = refs.pop(0) if two_src else None
    if with_mix:
        mp_ref, ms_ref, wout_ref, gmix_ref = refs[:4]
        refs = refs[4:]
    gpre_ref, wup_ref, wdn_ref, gpost_ref = refs[:4]
    refs = refs[4:]
    gfin_ref = refs.pop(0) if final else None
    if final:
        yp_ref, ys_ref, a_ref = refs
    else:
        o_ref, a_ref = refs

    i = pl.program_id(0)
    is_prompt = i < nbp
    x = jnp.where(is_prompt, xp_ref[...], xs_ref[...]) if two_src else xp_ref[...]
    if with_mix:
        m = jnp.where(is_prompt, mp_ref[...], ms_ref[...])
        x = x + _rms(jnp.dot(m, wout_ref[...], preferred_element_type=F32), gmix_ref[...])
    h = _rms(x, gpre_ref[...]).astype(BF16)
    for j in range(D_FF // FF_TILE):
        g = jnp.dot(h, wup_ref[:, j * FF_TILE:(j + 1) * FF_TILE], preferred_element_type=F32)
        u = jnp.dot(h, wup_ref[:, D_FF + j * FF_TILE:D_FF + (j + 1) * FF_TILE], preferred_element_type=F32)
        a_ref[:, j * FF_TILE:(j + 1) * FF_TILE] = (_silu(g) * u).astype(BF16)
    y = jnp.dot(a_ref[...], wdn_ref[...], preferred_element_type=F32)
    out = x + 0.5 * _rms(y, gpost_ref[...])
    if final:
        out = _rms(out, gfin_ref[...])

        @pl.when(is_prompt)
        def _():
            yp_ref[...] = out

        @pl.when(jnp.logical_not(is_prompt))
        def _():
            ys_ref[...] = out
    else:
        o_ref[...] = out


def _ffn(xs, layer, gpre, wup, wdn, gpost, *, n_prompt, n_sample, mix=None, gfin=None):
    nbp, nbs = n_prompt // TM, n_sample // TM
    two_src = len(xs) == 2
    final = gfin is not None
    p_spec = pl.BlockSpec((TM, D_MODEL), lambda i: (jnp.minimum(i, nbp - 1), 0))
    s_spec = pl.BlockSpec((TM, D_MODEL), lambda i: (jnp.maximum(i - nbp, 0), 0))
    all_spec = pl.BlockSpec((TM, D_MODEL), lambda i: (i, 0))
    args, in_specs = list(xs), ([p_spec, s_spec] if two_src else [all_spec])
    if mix is not None:
        mp, ms, wout, gmix = mix
        args += [mp, ms, wout, gmix]
        in_specs += [p_spec, s_spec, _layer_spec(wout.shape, layer, single=True), _layer_spec(gmix.shape, layer)]
    args += [gpre, wup, wdn, gpost]
    in_specs += [_layer_spec(gpre.shape, layer), _layer_spec(wup.shape, layer, single=True),
                 _layer_spec(wdn.shape, layer, single=True), _layer_spec(gpost.shape, layer)]
    if final:
        args.append(gfin)
        in_specs.append(pl.BlockSpec(gfin.shape, lambda i: (0, 0)))
        out_specs = [p_spec, s_spec]
        out_shape = [jax.ShapeDtypeStruct((n_prompt, D_MODEL), F32), jax.ShapeDtypeStruct((n_sample, D_MODEL), F32)]
    else:
        out_specs = all_spec
        out_shape = jax.ShapeDtypeStruct((n_prompt + n_sample, D_MODEL), F32)
    return pl.pallas_call(
        functools.partial(_ffn_kernel, nbp=nbp, two_src=two_src, with_mix=mix is not None, final=final),
        grid=(nbp + nbs,),
        in_specs=in_specs,
        out_specs=out_specs,
        out_shape=out_shape,
        scratch_shapes=[pltpu.VMEM((TM, D_FF), BF16)],
        compiler_params=_cparams(1),
        name="ffn",
    )(*args)


def _prep_kernel(x_ref, gpre_ref, wmain_ref, wscal_ref, cw_ref, cb_ref, cst_ref, wrg_ref, brg_ref,
                 lam_ref, alog_ref, dtb_ref, *rest, ns, t, aliased):
    if aliased:
        rest = rest[1:]
    a_ref, b_ref, rgate_ref, q_ref, k_ref, v_ref, zg_ref, scal_ref, ncst_ref, cscr_ref = rest
    r = ns * t
    ti = pl.program_id(1)
    h = _rms(x_ref[...], gpre_ref[...]).astype(BF16)
    proj = jnp.dot(h, wmain_ref[...], preferred_element_type=F32)
    scal = jnp.dot(h, wscal_ref[...], preferred_element_type=F32)

    @pl.when(ti == 0)
    def _():
        cscr_ref[:, CONV_PAD - (CONV_WIDTH - 1):CONV_PAD, :] = cst_ref[...]

    cscr_ref[:, CONV_PAD:CONV_PAD + t, :] = proj[:, :CONV_CH].reshape(ns, t, CONV_CH)
    conv = None
    for j in range(CONV_WIDTH):
        lo = CONV_PAD - (CONV_WIDTH - 1) + j
        term = cscr_ref[:, lo:lo + t, :] * cw_ref[j:j + 1, :]
        conv = term if conv is None else conv + term
    hist = cscr_ref[:, CONV_PAD + t - (CONV_WIDTH - 1):CONV_PAD + t, :]
    ncst_ref[...] = hist
    cscr_ref[:, CONV_PAD - (CONV_WIDTH - 1):CONV_PAD, :] = hist
    conv = conv.reshape(r, CONV_CH)

    xr = conv[:, :RG_WIDTH] + cb_ref[...]
    gax = jnp.dot(xr.astype(BF16), wrg_ref[...], preferred_element_type=F32) + brg_ref[...]
    rr = jax.nn.sigmoid(gax[:, :RG_WIDTH])
    ii = jax.nn.sigmoid(gax[:, RG_WIDTH:])
    log_a = -RG_C * rr * _softplus(-lam_ref[...])
    a = jnp.exp(log_a)
    a_ref[...] = a.reshape(ns, t, RG_WIDTH)
    one_minus_a2 = -jnp.tanh(log_a) * (a * a + 1.0)
    b_ref[...] = (jnp.sqrt(one_minus_a2) * (ii * xr)).reshape(ns, t, RG_WIDTH)
    rgate_ref[...] = jax.nn.gelu(proj[:, CONV_CH:CONV_CH + RG_WIDTH]).reshape(ns, t, RG_WIDTH)

    qkv = _silu(conv[:, RG_WIDTH:])
    for hh in range(DN_HEADS):
        sl = slice(hh * DN_HEAD_DIM, (hh + 1) * DN_HEAD_DIM)
        qh = qkv[:, sl]
        kh = qkv[:, DN_WIDTH + hh * DN_HEAD_DIM:DN_WIDTH + (hh + 1) * DN_HEAD_DIM]
        qn = qh * lax.rsqrt(jnp.sum(qh * qh, axis=-1, keepdims=True) + EPS) * (DN_HEAD_DIM ** -0.5)
        kn = kh * lax.rsqrt(jnp.sum(kh * kh, axis=-1, keepdims=True) + EPS)
        q_ref[:, :, sl] = qn.reshape(ns, t, DN_HEAD_DIM)
        k_ref[:, :, sl] = kn.reshape(ns, t, DN_HEAD_DIM)
    v_ref[...] = qkv[:, 2 * DN_WIDTH:].reshape(ns, t, DN_WIDTH)
    zg_ref[...] = _silu(proj[:, CONV_CH + RG_WIDTH:]).reshape(ns, t, DN_WIDTH)

    lane = lax.broadcasted_iota(jnp.int32, scal.shape, 1)
    beta = jax.nn.sigmoid(scal)
    g = -jnp.exp(alog_ref[...]) * _softplus(scal + dtb_ref[...])
    scal_ref[...] = jnp.where(lane < DN_HEADS, beta, g).reshape(ns, t, LANES)


def _prep(x_all, row0, bn, l, layer, conv_state, prev_conv, w, *, ns, t):
    assert ns == 1 or t == l
    r = ns * t
    blk0 = row0 // r
    grid = (bn // ns, l // t)
    row = lambda c: pl.BlockSpec((ns, t, c), lambda b, i: (b, i, 0))
    outs = [RG_WIDTH, RG_WIDTH, RG_WIDTH, DN_WIDTH, DN_WIDTH, DN_WIDTH, DN_WIDTH, LANES]
    st_spec = pl.BlockSpec((None, ns, CONV_WIDTH - 1, CONV_CH), lambda b, i: (layer, b, 0, 0))
    names = ["gpre", "wmain", "wscal", "cw", "cb"]
    names2 = ["wrg", "brg", "lam", "alog", "dtb"]
    args = [x_all] + [w[n] for n in names] + [conv_state] + [w[n] for n in names2]
    in_specs = ([pl.BlockSpec((r, D_MODEL), lambda b, i: (blk0 + b * (l // t) + i, 0))]
                + [_layer_spec(w[n].shape, layer) for n in names] + [st_spec]
                + [_layer_spec(w[n].shape, layer) for n in names2])
    aliases = {}
    if prev_conv is not None:
        aliases = {len(args): len(outs)}
        args.append(prev_conv)
        in_specs.append(pl.BlockSpec(memory_space=pl.ANY))
    return pl.pallas_call(
        functools.partial(_prep_kernel, ns=ns, t=t, aliased=prev_conv is not None),
        grid=grid,
        in_specs=in_specs,
        out_specs=[row(c) for c in outs] + [st_spec],
        out_shape=[jax.ShapeDtypeStruct((bn, l, c), F32) for c in outs]
        + [jax.ShapeDtypeStruct(conv_state.shape, F32)],
        scratch_shapes=[pltpu.VMEM((ns, t + CONV_PAD, CONV_CH), F32)],
        input_output_aliases=aliases,
        compiler_params=_cparams(2),
        name="mixer_prep",
    )(*args)


def _recur_kernel(a_ref, b_ref, rgate_ref, q_ref, k_ref, v_ref, zg_ref, scal_ref, h0_ref, s0_ref, nw_ref,
                  *rest, ns, t, cpi, aliased):
    if aliased:
        rest = rest[2:]
    mix_ref, hout_ref, sout_ref, ascr_ref, bscr_ref = rest
    r = ns * t
    seg = min(CHUNK, t)
    nseg = CHUNK // seg
    ti = pl.program_id(1)

    @pl.when(ti == 0)
    def _():
        hout_ref[...] = h0_ref[...]
        sout_ref[...] = s0_ref[...]

    a = a_ref[...].reshape(r, RG_WIDTH)
    b = b_ref[...].reshape(r, RG_WIDTH)
    rowmod = lax.broadcasted_iota(jnp.int32, (r, RG_WIDTH), 0) & (SUBLANES - 1)
    for s in (1, 2, 4):
        m = rowmod >= s
        a_sh = pltpu.roll(a, s, axis=0)
        b_sh = pltpu.roll(b, s, axis=0)
        b = jnp.where(m, a * b_sh + b, b)
        a = jnp.where(m, a * a_sh, a)
    ascr_ref[...] = a.reshape(ns, t, RG_WIDTH)
    bscr_ref[...] = b.reshape(ns, t, RG_WIDTH)

    def rg_body(gi, hc):
        rows = pl.ds(pl.multiple_of(gi * SUBLANES, SUBLANES), SUBLANES)
        hb = ascr_ref[:, rows, :] * hc + bscr_ref[:, rows, :]
        mix_ref[:, rows, :RG_WIDTH] = (hb * rgate_ref[:, rows, :]).astype(mix_ref.dtype)
        return hb[:, SUBLANES - 1:SUBLANES, :]

    hout_ref[...] = lax.fori_loop(0, t // SUBLANES, rg_body, hout_ref[...])

    ri = lax.broadcasted_iota(jnp.int32, (CHUNK, CHUNK), 0)
    ci = lax.broadcasted_iota(jnp.int32, (CHUNK, CHUNK), 1)
    seg_shift = seg.bit_length() - 1
    same = (ri >> seg_shift) == (ci >> seg_shift)
    incl = jnp.logical_and(same, ri >= ci)
    strict = jnp.logical_and(same, ri > ci)
    incl_f = incl.astype(F32)
    eye = (ri == ci).astype(F32)
    sel = (lax.broadcasted_iota(jnp.int32, (SUBLANES, LANES), 0)
           == lax.broadcasted_iota(jnp.int32, (SUBLANES, LANES), 1)).astype(F32)
    nw = nw_ref[...]

    def chunk_index(c):
        if t >= CHUNK:
            per_seq = t // CHUNK
            return c // per_seq, pl.ds(pl.multiple_of((c % per_seq) * CHUNK, CHUNK), CHUNK)
        per_chunk = CHUNK // t
        return pl.ds(pl.multiple_of(c * per_chunk, per_chunk), per_chunk), slice(None)

    def load_rows(ref, c, lanes):
        i0, i1 = chunk_index(c)
        return ref[i0, i1, lanes].reshape(CHUNK, -1)

    def store_rows(ref, c, lanes, val):
        i0, i1 = chunk_index(c)
        if t >= CHUNK:
            ref[i0, i1, lanes] = val
        else:
            ref[i0, i1, lanes] = val.reshape(CHUNK // t, t, -1)

    def state_index(c, jj):
        return c // (t // CHUNK) if t >= CHUNK else c * nseg + jj

    def group_body(gi, carry):
        chains = []
        for cc in range(cpi):
            c = gi * cpi + cc
            sc = load_rows(scal_ref, c, slice(None))
            gall = _fdot(incl_f, sc)
            gt = _fdot_nt(sel, gall)
            for hh in range(DN_HEADS):
                sl = slice(hh * DN_HEAD_DIM, (hh + 1) * DN_HEAD_DIM)
                chains.append(dict(
                    c=c, hh=hh, sl=sl, beta=sc[:, hh:hh + 1],
                    gc=gall[:, DN_HEADS + hh:DN_HEADS + hh + 1],
                    gr=gt[DN_HEADS + hh:DN_HEADS + hh + 1, :]))
        for ch in chains:
            qh = load_rows(q_ref, ch["c"], ch["sl"])
            kh = load_rows(k_ref, ch["c"], ch["sl"])
            dec = jnp.where(incl, jnp.exp(jnp.where(incl, ch["gc"] - ch["gr"], 0.0)), 0.0)
            kb = kh * ch["beta"]
            ch.update(qh=qh, kh=kh, kb=kb,
                      x=jnp.where(strict, _bdot_nt(kb, kh) * dec, 0.0),
                      att=_bdot_nt(qh, kh) * dec)
        for ch in chains:
            ch["p"] = eye - ch["x"]
        span = 2
        while span < seg:
            for ch in chains:
                ch["x"] = _bdot(ch["x"], ch["x"])
            for ch in chains:
                ch["p"] = ch["p"] + _bdot(ch["p"], ch["x"])
            span *= 2
        for ch in chains:
            vh = load_rows(v_ref, ch["c"], ch["sl"])
            eg = jnp.exp(ch["gc"])
            sol = _bdot(ch["p"], jnp.concatenate([vh * ch["beta"], ch["kb"] * eg], axis=1))
            ch.update(u=sol[:, :DN_HEAD_DIM], w=sol[:, DN_HEAD_DIM:], qe=ch["qh"] * eg)
        for ch in chains:
            ch["s_old"], ch["rr"] = [], []
            for jj in range(nseg):
                rs = slice(jj * seg, (jj + 1) * seg)
                s_old = sout_ref[state_index(ch["c"], jj), ch["hh"]]
                ch["s_old"].append(s_old)
                ch["rr"].append(_bdot(jnp.concatenate([ch["w"][rs], ch["qe"][rs]], axis=0), s_old))
        for ch in chains:
            vnews = []
            for jj in range(nseg):
                rs = slice(jj * seg, (jj + 1) * seg)
                vnew = ch["u"][rs] - ch["rr"][jj][:seg]
                gl = ch["gc"][(jj + 1) * seg - 1:(jj + 1) * seg, :]
                kdec = ch["kh"][rs] * jnp.exp(gl - ch["gc"][rs])
                sout_ref[state_index(ch["c"], jj), ch["hh"]] = (
                    ch["s_old"][jj] * jnp.exp(gl) + _bdot(kdec.T, vnew))
                vnews.append(vnew)
            ch["vnew"] = vnews[0] if nseg == 1 else jnp.concatenate(vnews, axis=0)
        for ch in chains:
            oint = ch["rr"][0][seg:] if nseg == 1 else jnp.concatenate([x[seg:] for x in ch["rr"]], axis=0)
            o = oint + _bdot(ch["att"], ch["vnew"])
            o = _rms(o, nw) * load_rows(zg_ref, ch["c"], ch["sl"])
            store_rows(mix_ref, ch["c"], slice(RG_WIDTH + ch["hh"] * DN_HEAD_DIM,
                                               RG_WIDTH + (ch["hh"] + 1) * DN_HEAD_DIM), o.astype(mix_ref.dtype))
        return carry

    lax.fori_loop(0, r // (CHUNK * cpi), group_body, 0)


def _recur(prep_out, layer, h0, s0, prev_h, prev_s, nw, *, ns, t):
    a = prep_out[0]
    bn, l, _ = a.shape
    grid = (bn // ns, l // t)
    cpi = min(8, ns * t // CHUNK)
    row = lambda c: pl.BlockSpec((ns, t, c), lambda bi, i: (bi, i, 0))
    h_spec = pl.BlockSpec((None, ns, 1, RG_WIDTH), lambda bi, i: (layer, bi, 0, 0))
    s_spec = pl.BlockSpec((None, ns, DN_HEADS, DN_HEAD_DIM, DN_HEAD_DIM), lambda bi, i: (layer, bi, 0, 0, 0))
    args = list(prep_out) + [h0, s0, nw]
    in_specs = ([row(RG_WIDTH)] * 3 + [row(DN_WIDTH)] * 4
                + [row(LANES), h_spec, s_spec, _layer_spec(nw.shape, layer)])
    aliases = {}
    if prev_h is not None:
        aliases = {len(args): 1, len(args) + 1: 2}
        args += [prev_h, prev_s]
        in_specs += [pl.BlockSpec(memory_space=pl.ANY)] * 2
    return pl.pallas_call(
        functools.partial(_recur_kernel, ns=ns, t=t, cpi=cpi, aliased=prev_h is not None),
        grid=grid,
        in_specs=in_specs,
        out_specs=[row(D_MODEL), h_spec, s_spec],
        out_shape=[
            jax.ShapeDtypeStruct((bn, l, D_MODEL), BF16),
            jax.ShapeDtypeStruct(h0.shape, F32),
            jax.ShapeDtypeStruct(s0.shape, F32),
        ],
        scratch_shapes=[pltpu.VMEM((ns, t, RG_WIDTH), F32), pltpu.VMEM((ns, t, RG_WIDTH), F32)],
        input_output_aliases=aliases,
        compiler_params=_cparams(2),
        name="mixer_recur",
    )(*args)


def _lane_pad(mat, offset):
    depth, n = mat.shape
    out = jnp.zeros((depth, 1, LANES), F32)
    return lax.dynamic_update_slice(out, mat.reshape(depth, 1, n).astype(F32), (0, 0, offset))


def _weights(ffn1_norm_pre, ffn1_w_up, ffn1_w_down, ffn1_norm_post, mix_norm_pre, w_in, conv_w, conv_b_rg,
             rg_w_a, rg_b_a, rg_w_x, rg_b_x, rg_lambda, dn_a_log, dn_dt_bias, dn_norm_w, w_out, mix_norm_post,
             ffn2_norm_pre, ffn2_w_up, ffn2_w_down, ffn2_norm_post):
    depth = w_in.shape[0]
    row = lambda v: v.reshape(depth, 1, -1)
    wscal = jnp.zeros((depth, D_MODEL, LANES), BF16).at[:, :, :2 * DN_HEADS].set(
        w_in[:, :, MAIN_COLS:].astype(BF16))
    eye = jnp.eye(RG_BLOCKS, dtype=F32)
    bd = lambda w: jnp.einsum('lhij,hg->lhigj', w, eye).reshape(depth, RG_WIDTH, RG_WIDTH)
    return dict(
        f1=(row(ffn1_norm_pre), ffn1_w_up.astype(BF16), ffn1_w_down.astype(BF16), row(ffn1_norm_post)),
        f2=(row(ffn2_norm_pre), ffn2_w_up.astype(BF16), ffn2_w_down.astype(BF16), row(ffn2_norm_post)),
        prep=dict(gpre=row(mix_norm_pre), wmain=w_in[:, :, :MAIN_COLS].astype(BF16), wscal=wscal, cw=conv_w,
                  cb=row(conv_b_rg), wrg=jnp.concatenate([bd(rg_w_a), bd(rg_w_x)], axis=2).astype(BF16),
                  brg=jnp.concatenate([row(rg_b_a), row(rg_b_x)], axis=2), lam=row(rg_lambda),
                  alog=_lane_pad(dn_a_log, DN_HEADS), dtb=_lane_pad(dn_dt_bias, DN_HEADS)),
        nw=row(dn_norm_w),
        wout=w_out.astype(BF16),
        gmix=row(mix_norm_post),
    )


def kernel(x_prompt, x_sample, state_conv, state_rglru, state_delta, ffn1_norm_pre, ffn1_w_up, ffn1_w_down, ffn1_norm_post, mix_norm_pre, w_in, conv_w, conv_b_rg, rg_w_a, rg_b_a, rg_w_x, rg_b_x, rg_lambda, dn_a_log, dn_dt_bias, dn_norm_w, w_out, mix_norm_post, ffn2_norm_pre, ffn2_w_up, ffn2_w_down, ffn2_norm_post, final_norm):
    depth = w_in.shape[0]
    w = _weights(ffn1_norm_pre, ffn1_w_up, ffn1_w_down, ffn1_norm_post, mix_norm_pre, w_in, conv_w, conv_b_rg,
                 rg_w_a, rg_b_a, rg_w_x, rg_b_x, rg_lambda, dn_a_log, dn_dt_bias, dn_norm_w, w_out,
                 mix_norm_post, ffn2_norm_pre, ffn2_w_up, ffn2_w_down, ffn2_norm_post)
    bp, lp, _ = x_prompt.shape
    bs, ls, _ = x_sample.shape
    n_p, n_s = bp * lp, bs * ls
    dt = x_prompt.dtype
    pops = [
        dict(row0=0, bn=bp, l=lp, prep_blk=(1, TM), rec_blk=(bp, CHUNK),
             conv=jnp.zeros((depth, bp, CONV_WIDTH - 1, CONV_CH), dt),
             rg=jnp.zeros((depth, bp, 1, RG_WIDTH), dt),
             dn=jnp.zeros((depth, bp, DN_HEADS, DN_HEAD_DIM, DN_HEAD_DIM), dt)),
        dict(row0=n_p, bn=bs, l=ls, prep_blk=(TM // ls, ls), rec_blk=(2 * CHUNK // ls, ls),
             conv=state_conv, rg=state_rglru.reshape(depth, bs, 1, RG_WIDTH), dn=state_delta),
    ]
    for p in pops:
        p.update(new_conv=None, new_rg=None, new_dn=None)
    kw = dict(n_prompt=n_p, n_sample=n_s)
    x_all = None
    mix = None
    for li in range(depth):
        xs = [x_prompt.reshape(n_p, D_MODEL), x_sample.reshape(n_s, D_MODEL)] if li == 0 else [x_all]
        x_all = _ffn(xs, li, *w["f1"], mix=mix, **kw)
        mixed = []
        for p in pops:
            *prep_out, p["new_conv"] = _prep(x_all, p["row0"], p["bn"], p["l"], li, p["conv"], p["new_conv"],
                                             w["prep"], ns=p["prep_blk"][0], t=p["prep_blk"][1])
            m, p["new_rg"], p["new_dn"] = _recur(prep_out, li, p["rg"], p["dn"], p["new_rg"], p["new_dn"],
                                                 w["nw"], ns=p["rec_blk"][0], t=p["rec_blk"][1])
            mixed.append(m.reshape(p["bn"] * p["l"], D_MODEL))
        mix = (mixed[0], mixed[1], w["wout"], w["gmix"])
        if li == depth - 1:
            y_p, y_s = _ffn([x_all], li, *w["f2"], mix=mix, gfin=final_norm.reshape(1, -1), **kw)
        else:
            x_all = _ffn([x_all], li, *w["f2"], mix=mix, **kw)
            mix = None
    pp, ps = pops
    return (y_p.reshape(bp, lp, D_MODEL), y_s.reshape(bs, ls, D_MODEL),
            pp["new_conv"], pp["new_rg"].reshape(depth, bp, RG_WIDTH), pp["new_dn"],
            ps["new_conv"], ps["new_rg"].reshape(depth, bs, RG_WIDTH), ps["new_dn"])
```

```python
import functools

import jax
import jax.numpy as jnp
from jax import lax
from jax.experimental import pallas as pl
from jax.experimental.pallas import tpu as pltpu

F32 = jnp.float32
BF16 = jnp.bfloat16

D_MODEL = 1024
D_FF = 2816
RG_WIDTH = 512
RG_BLOCKS = 8
RG_BLOCK = RG_WIDTH // RG_BLOCKS
RG_C = 8.0
DN_HEADS = 4
DN_HEAD_DIM = 128
DN_WIDTH = DN_HEADS * DN_HEAD_DIM
CONV_WIDTH = 4
CONV_CH = RG_WIDTH + 3 * DN_WIDTH
GATE_COLS = RG_WIDTH + DN_WIDTH
MAIN_COLS = CONV_CH + GATE_COLS
EPS = 1e-6

LANES = 128
SUBLANES = 8
CHUNK = 64
CONV_PAD = SUBLANES
FF_TILE = 256
VMEM_LIMIT = 56 * 1024 * 1024
PREP_WIDTHS = (RG_WIDTH, RG_WIDTH, RG_WIDTH, DN_WIDTH, DN_WIDTH, DN_WIDTH, DN_WIDTH, LANES)


def _rms(x, g):
    return x * lax.rsqrt(jnp.mean(x * x, axis=-1, keepdims=True) + EPS) * g


def _silu(x):
    return x * jax.nn.sigmoid(x)


def _softplus(x):
    return jnp.maximum(x, 0.0) + jnp.log1p(jnp.exp(-jnp.abs(x)))


def _bdot(a, b):
    return jnp.dot(a.astype(BF16), b.astype(BF16), preferred_element_type=F32)


def _bdot_nt(a, b):
    return lax.dot_general(a.astype(BF16), b.astype(BF16), (((1,), (1,)), ((), ())),
                           preferred_element_type=F32)


def _fdot(a, b):
    return jnp.dot(a, b, preferred_element_type=F32, precision=lax.Precision.HIGHEST)


def _fdot_nt(a, b):
    return lax.dot_general(a, b, (((1,), (1,)), ((), ())), preferred_element_type=F32,
                           precision=lax.Precision.HIGHEST)


def _layer_spec(shape, layer, *, single=False):
    nd = len(shape)
    kw = dict(pipeline_mode=pl.Buffered(1)) if single else {}
    return pl.BlockSpec((None,) + tuple(shape[1:]), lambda *_: (layer,) + (0,) * (nd - 1), **kw)


class _Tiling:
    def __init__(self, bn, l, ns, t):
        assert bn % ns == 0 and l % t == 0
        self.ns, self.t, self.r = ns, t, ns * t
        self.nt = l // t
        self.n = (bn // ns) * self.nt

    def tile_a(self, g):
        return jnp.minimum(g, self.n - 1)

    def tile_b(self, g):
        return jnp.maximum(g - 1, 0)

    def rows(self, width, stage):
        tile = self.tile_a if stage == "a" else self.tile_b
        return pl.BlockSpec((self.ns, self.t, width),
                            lambda g: (tile(g) // self.nt, tile(g) % self.nt, 0))

    def state(self, shape, layer, stage):
        tile = self.tile_a if stage == "a" else self.tile_b
        nd = len(shape)
        return pl.BlockSpec((None, self.ns) + tuple(shape[2:]),
                            lambda g: (layer, tile(g) // self.nt) + (0,) * (nd - 2))


def _cparams():
    return pltpu.CompilerParams(dimension_semantics=("arbitrary",), vmem_limit_bytes=VMEM_LIMIT)


def _interleave(*stages):
    live = list(stages)
    while live:
        for st in list(live):
            try:
                next(st)
            except StopIteration:
                live.remove(st)


def _ffn_steps(x_fn, gpre_ref, wup_ref, wdn_ref, gpost_ref, hid_ref, finish):
    x = x_fn()
    h = _rms(x, gpre_ref[...]).astype(BF16)
    yield
    for j in range(D_FF // FF_TILE):
        g = jnp.dot(h, wup_ref[:, j * FF_TILE:(j + 1) * FF_TILE], preferred_element_type=F32)
        u = jnp.dot(h, wup_ref[:, D_FF + j * FF_TILE:D_FF + (j + 1) * FF_TILE], preferred_element_type=F32)
        hid_ref[:, j * FF_TILE:(j + 1) * FF_TILE] = (_silu(g) * u).astype(BF16)
        yield
    ys = []
    for c in range(D_MODEL // FF_TILE):
        ys.append(jnp.dot(hid_ref[...], wdn_ref[:, c * FF_TILE:(c + 1) * FF_TILE], preferred_element_type=F32))
        yield
    finish(x + 0.5 * _rms(jnp.concatenate(ys, axis=1), gpost_ref[...]))


def _prep_steps(x_fn, w, outs, ncst_ref, cscr_ref, ns, t):
    gpre_ref, wmain_ref, wscal_ref, cw_ref, cb_ref, wrg_ref, brg_ref, lam_ref, alog_ref, dtb_ref = w
    a_ref, b_ref, rgate_ref, q_ref, k_ref, v_ref, zg_ref, scal_ref = outs
    r = ns * t
    cw_n = FF_TILE
    h = _rms(x_fn(), gpre_ref[...]).astype(BF16)
    yield

    def proj_cols(c0):
        return jnp.dot(h, wmain_ref[:, c0:c0 + cw_n], preferred_element_type=F32)

    def conv_cols(c0):
        cols = slice(c0, c0 + cw_n)
        cscr_ref[:, CONV_PAD:CONV_PAD + t, cols] = proj_cols(c0).reshape(ns, t, cw_n)
        conv = None
        for j in range(CONV_WIDTH):
            lo = CONV_PAD - (CONV_WIDTH - 1) + j
            term = cscr_ref[:, lo:lo + t, cols] * cw_ref[j:j + 1, cols]
            conv = term if conv is None else conv + term
        hist = cscr_ref[:, CONV_PAD + t - (CONV_WIDTH - 1):CONV_PAD + t, cols]
        ncst_ref[:, :, cols] = hist
        cscr_ref[:, CONV_PAD - (CONV_WIDTH - 1):CONV_PAD, cols] = hist
        return conv.reshape(r, cw_n)

    xr_parts = []
    for c0 in range(0, RG_WIDTH, cw_n):
        xr_parts.append(conv_cols(c0) + cb_ref[:, c0:c0 + cw_n])
        yield
    xr = jnp.concatenate(xr_parts, axis=1)
    gax = jnp.dot(xr.astype(BF16), wrg_ref[...], preferred_element_type=F32) + brg_ref[...]
    rr = jax.nn.sigmoid(gax[:, :RG_WIDTH])
    ii = jax.nn.sigmoid(gax[:, RG_WIDTH:])
    log_a = -RG_C * rr * _softplus(-lam_ref[...])
    a = jnp.exp(log_a)
    a_ref[...] = a.reshape(ns, t, RG_WIDTH)
    one_minus_a2 = -jnp.tanh(log_a) * (a * a + 1.0)
    b_ref[...] = (jnp.sqrt(one_minus_a2) * (ii * xr)).reshape(ns, t, RG_WIDTH)
    yield

    for base, ref, scale in ((RG_WIDTH, q_ref, DN_HEAD_DIM ** -0.5), (RG_WIDTH + DN_WIDTH, k_ref, None)):
        for c0 in range(0, DN_WIDTH, cw_n):
            act = _silu(conv_cols(base + c0))
            for hh in range(cw_n // DN_HEAD_DIM):
                xh = act[:, hh * DN_HEAD_DIM:(hh + 1) * DN_HEAD_DIM]
                xn = xh * lax.rsqrt(jnp.sum(xh * xh, axis=-1, keepdims=True) + EPS)
                if scale is not None:
                    xn = xn * scale
                lo = c0 + hh * DN_HEAD_DIM
                ref[:, :, lo:lo + DN_HEAD_DIM] = xn.reshape(ns, t, DN_HEAD_DIM)
            yield
    for c0 in range(0, DN_WIDTH, cw_n):
        v_ref[:, :, c0:c0 + cw_n] = _silu(conv_cols(RG_WIDTH + 2 * DN_WIDTH + c0)).reshape(ns, t, cw_n)
        yield

    for c0 in range(0, RG_WIDTH, cw_n):
        rgate_ref[:, :, c0:c0 + cw_n] = jax.nn.gelu(proj_cols(CONV_CH + c0)).reshape(ns, t, cw_n)
        yield
    for c0 in range(0, DN_WIDTH, cw_n):
        zg_ref[:, :, c0:c0 + cw_n] = _silu(proj_cols(CONV_CH + RG_WIDTH + c0)).reshape(ns, t, cw_n)
        yield

    scal = jnp.dot(h, wscal_ref[...], preferred_element_type=F32)
    lane = lax.broadcasted_iota(jnp.int32, scal.shape, 1)
    beta = jax.nn.sigmoid(scal)
    g = -jnp.exp(alog_ref[...]) * _softplus(scal + dtb_ref[...])
    scal_ref[...] = jnp.where(lane < DN_HEADS, beta, g).reshape(ns, t, LANES)


def _recur_steps(ins, nw_ref, hout_ref, sout_ref, mix_ref, valid, ns, t):
    a_ref, b_ref, rgate_ref, q_ref, k_ref, v_ref, zg_ref, scal_ref = ins
    r = ns * t
    seg = min(CHUNK, t)
    nseg = CHUNK // seg
    nchunk = r // CHUNK

    a = a_ref[...].reshape(r, RG_WIDTH)
    b = b_ref[...].reshape(r, RG_WIDTH)
    rowmod = lax.broadcasted_iota(jnp.int32, (r, RG_WIDTH), 0) & (SUBLANES - 1)
    for s in (1, 2, 4):
        m = rowmod >= s
        a_sh = pltpu.roll(a, s, axis=0)
        b_sh = pltpu.roll(b, s, axis=0)
        b = jnp.where(m, a * b_sh + b, b)
        a = jnp.where(m, a * a_sh, a)
        yield
    a = a.reshape(ns, t, RG_WIDTH)
    b = b.reshape(ns, t, RG_WIDTH)
    h_old = hout_ref[...]
    hc = h_old
    for gi in range(t // SUBLANES):
        rows = slice(gi * SUBLANES, (gi + 1) * SUBLANES)
        hb = a[:, rows, :] * hc + b[:, rows, :]
        mix_ref[:, rows, :RG_WIDTH] = (hb * rgate_ref[:, rows, :]).astype(mix_ref.dtype)
        hc = hb[:, SUBLANES - 1:SUBLANES, :]
    hout_ref[...] = jnp.where(valid, hc, h_old)
    yield

    ri = lax.broadcasted_iota(jnp.int32, (CHUNK, CHUNK), 0)
    ci = lax.broadcasted_iota(jnp.int32, (CHUNK, CHUNK), 1)
    seg_shift = seg.bit_length() - 1
    same = (ri >> seg_shift) == (ci >> seg_shift)
    incl = jnp.logical_and(same, ri >= ci)
    strict = jnp.logical_and(same, ri > ci)
    incl_f = incl.astype(F32)
    eye = (ri == ci).astype(F32)
    sel = (lax.broadcasted_iota(jnp.int32, (SUBLANES, LANES), 0)
           == lax.broadcasted_iota(jnp.int32, (SUBLANES, LANES), 1)).astype(F32)
    nw = nw_ref[...]

    def chunk_index(c):
        if t >= CHUNK:
            per_seq = t // CHUNK
            return c // per_seq, slice((c % per_seq) * CHUNK, (c % per_seq + 1) * CHUNK)
        per_chunk = CHUNK // t
        return slice(c * per_chunk, (c + 1) * per_chunk), slice(None)

    def load_rows(ref, c, lanes):
        i0, i1 = chunk_index(c)
        return ref[i0, i1, lanes].reshape(CHUNK, -1)

    def store_rows(ref, c, lanes, val):
        i0, i1 = chunk_index(c)
        ref[i0, i1, lanes] = val if t >= CHUNK else val.reshape(CHUNK // t, t, -1)

    def state_index(c, jj):
        return c // (t // CHUNK) if t >= CHUNK else c * nseg + jj

    chains = []
    for c in range(nchunk):
        sc = load_rows(scal_ref, c, slice(None))
        gall = _fdot(incl_f, sc)
        gt = _fdot_nt(sel, gall)
        for hh in range(DN_HEADS):
            sl = slice(hh * DN_HEAD_DIM, (hh + 1) * DN_HEAD_DIM)
            chains.append(dict(
                c=c, hh=hh, sl=sl, beta=sc[:, hh:hh + 1],
                gc=gall[:, DN_HEADS + hh:DN_HEADS + hh + 1],
                gr=gt[DN_HEADS + hh:DN_HEADS + hh + 1, :]))
    yield
    for ch in chains:
        qh = load_rows(q_ref, ch["c"], ch["sl"])
        kh = load_rows(k_ref, ch["c"], ch["sl"])
        dec = jnp.where(incl, jnp.exp(jnp.where(incl, ch["gc"] - ch["gr"], 0.0)), 0.0)
        kb = kh * ch["beta"]
        ch.update(qh=qh, kh=kh, kb=kb,
                  x=jnp.where(strict, _bdot_nt(kb, kh) * dec, 0.0),
                  att=_bdot_nt(qh, kh) * dec)
    for ch in chains:
        ch["p"] = eye - ch["x"]
    yield
    span = 2
    while span < seg:
        for ch in chains:
            ch["x"] = _bdot(ch["x"], ch["x"])
        yield
        for ch in chains:
            ch["p"] = ch["p"] + _bdot(ch["p"], ch["x"])
        yield
        span *= 2
    for ch in chains:
        vh = load_rows(v_ref, ch["c"], ch["sl"])
        eg = jnp.exp(ch["gc"])
        sol = _bdot(ch["p"], jnp.concatenate([vh * ch["beta"], ch["kb"] * eg], axis=1))
        ch.update(u=sol[:, :DN_HEAD_DIM], w=sol[:, DN_HEAD_DIM:], qe=ch["qh"] * eg)
    yield
    for ch in chains:
        ch["s_old"], ch["rr"] = [], []
        for jj in range(nseg):
            rs = slice(jj * seg, (jj + 1) * seg)
            s_old = sout_ref[state_index(ch["c"], jj), ch["hh"]]
            ch["s_old"].append(s_old)
            ch["rr"].append(_bdot(jnp.concatenate([ch["w"][rs], ch["qe"][rs]], axis=0), s_old))
    yield
    for ch in chains:
        vnews = []
        for jj in range(nseg):
            rs = slice(jj * seg, (jj + 1) * seg)
            vnew = ch["u"][rs] - ch["rr"][jj][:seg]
            gl = ch["gc"][(jj + 1) * seg - 1:(jj + 1) * seg, :]
            kdec = ch["kh"][rs] * jnp.exp(gl - ch["gc"][rs])
            s_new = ch["s_old"][jj] * jnp.exp(gl) + _bdot(kdec.T, vnew)
            sout_ref[state_index(ch["c"], jj), ch["hh"]] = jnp.where(valid, s_new, ch["s_old"][jj])
            vnews.append(vnew)
        ch["vnew"] = vnews[0] if nseg == 1 else jnp.concatenate(vnews, axis=0)
    yield
    for ch in chains:
        oint =ch["rr"][0][seg:] if nseg == 1 else jnp.concatenate([x[seg:] for x in ch["rr"]], axis=0)
        o = oint + _bdot(ch["att"], ch["vnew"])
        o = _rms(o, nw) * load_rows(zg_ref, ch["c"], ch["sl"])
        store_rows(mix_ref, ch["c"], slice(RG_WIDTH + ch["hh"] * DN_HEAD_DIM,
                                           RG_WIDTH + (ch["hh"] + 1) * DN_HEAD_DIM), o.astype(mix_ref.dtype))


def _kernel_a(x_ref, gpre1_ref, wup_ref, wdn_ref, gpost1_ref,
              gprem_ref, wmain_ref, wscal_ref, cw_ref, cb_ref, wrg_ref, brg_ref, lam_ref, alog_ref, dtb_ref,
              cst_ref, *rest, tl, aliased):
    if aliased:
        rest = rest[1:]
    x1_ref, *outs, ncst_ref, hid_ref, xs_ref, cscr_ref = rest
    g = pl.program_id(0)
    hist_rows = slice(CONV_PAD - (CONV_WIDTH - 1), CONV_PAD)

    @pl.when(g == 0)
    def _():
        xs_ref[...] = jnp.zeros_like(xs_ref)
        cscr_ref[...] = jnp.zeros_like(cscr_ref)

    @pl.when((g - 1) % tl.nt == 0)
    def _():
        cscr_ref[:, hist_rows, :] = cst_ref[...]

    def finish(x1):
        x1_ref[...] = x1.reshape(tl.ns, tl.t, D_MODEL)
        xs_ref[g % 2] = x1

    w = (gprem_ref, wmain_ref, wscal_ref, cw_ref, cb_ref, wrg_ref, brg_ref, lam_ref, alog_ref, dtb_ref)
    _interleave(
        _ffn_steps(lambda: x_ref[...].reshape(tl.r, D_MODEL), gpre1_ref, wup_ref, wdn_ref, gpost1_ref, hid_ref,
                   finish),
        _prep_steps(lambda: xs_ref[(g + 1) % 2], w, outs, ncst_ref, cscr_ref, tl.ns, tl.t))


def _call_a(x, layer, conv_state, prev_conv, w, *, ns, t):
    bn, l, _ = x.shape
    tl = _Tiling(bn, l, ns, t)
    f1, pw = w["f1"], w["prep"]
    big = {"wup", "wdn", "wmain", "wrg"}
    names = ["gpre", "wmain", "wscal", "cw", "cb", "wrg", "brg", "lam", "alog", "dtb"]
    args = [x, *f1] + [pw[n] for n in names] + [conv_state]
    in_specs = ([tl.rows(D_MODEL, "a"), _layer_spec(f1[0].shape, layer), _layer_spec(f1[1].shape, layer, single=True),
                 _layer_spec(f1[2].shape, layer, single=True), _layer_spec(f1[3].shape, layer)]
                + [_layer_spec(pw[n].shape, layer, single=n in big) for n in names]
                + [tl.state(conv_state.shape, layer, "b")])
    aliases = {}
    if prev_conv is not None:
        aliases = {len(args): 1 + len(PREP_WIDTHS)}
        args.append(prev_conv)
        in_specs.append(pl.BlockSpec(memory_space=pl.ANY))
    return pl.pallas_call(
        functools.partial(_kernel_a, tl=tl, aliased=prev_conv is not None),
        grid=(tl.n + 1,),
        in_specs=in_specs,
        out_specs=[tl.rows(D_MODEL, "a")] + [tl.rows(c, "b") for c in PREP_WIDTHS]
        + [tl.state(conv_state.shape, layer, "b")],
        out_shape=[jax.ShapeDtypeStruct((bn, l, D_MODEL), F32)]
        + [jax.ShapeDtypeStruct((bn, l, c), F32) for c in PREP_WIDTHS]
        + [jax.ShapeDtypeStruct(conv_state.shape, F32)],
        scratch_shapes=[pltpu.VMEM((tl.r, D_FF), BF16), pltpu.VMEM((2, tl.r, D_MODEL), F32),
                        pltpu.VMEM((ns, t + CONV_PAD, CONV_CH), F32)],
        input_output_aliases=aliases,
        compiler_params=_cparams(),
        name="ffn1_prep",
    )(*args)


def _kernel_b(a_ref, b_ref, rgate_ref, q_ref, k_ref, v_ref, zg_ref, scal_ref, h0_ref, s0_ref, nw_ref,
              x1_ref, wout_ref, gmix_ref, gpre2_ref, wup_ref, wdn_ref, gpost2_ref, *rest, tl, aliased, final):
    rest = list(rest)
    gfin_ref = rest.pop(0) if final else None
    if aliased:
        rest = rest[2:]
    y_ref, hout_ref, sout_ref, hid_ref, ms_ref = rest
    g = pl.program_id(0)
    valid = g < tl.n

    @pl.when(g == 0)
    def _():
        ms_ref[...] = jnp.zeros_like(ms_ref)

    @pl.when(jnp.logical_and(g % tl.nt == 0, valid))
    def _():
        hout_ref[...] = h0_ref[...]
        sout_ref[...] = s0_ref[...]

    ins = (a_ref, b_ref, rgate_ref, q_ref, k_ref, v_ref, zg_ref, scal_ref)

    def mixed_in():
        m = ms_ref[(g + 1) % 2].reshape(tl.r, D_MODEL)
        x = x1_ref[...].reshape(tl.r, D_MODEL)
        return x + _rms(jnp.dot(m.astype(BF16), wout_ref[...], preferred_element_type=F32), gmix_ref[...])

    def finish(out):
        if final:
            out = _rms(out, gfin_ref[...])
        y_ref[...] = out.reshape(tl.ns, tl.t, D_MODEL)

    _interleave(
        _recur_steps(ins, nw_ref, hout_ref, sout_ref, ms_ref.at[g % 2], valid, tl.ns, tl.t),
        _ffn_steps(mixed_in, gpre2_ref, wup_ref, wdn_ref, gpost2_ref, hid_ref, finish))


def _call_b(prep_out, x1, layer, h0, s0, prev_h, prev_s, w, gfin, *, ns, t):
    bn, l, _ = x1.shape
    tl = _Tiling(bn, l, ns, t)
    f2 = w["f2"]
    args = list(prep_out) + [h0, s0, w["nw"], x1, w["wout"], w["gmix"], *f2]
    in_specs = ([tl.rows(c, "a") for c in PREP_WIDTHS]
                + [tl.state(h0.shape, layer, "a"), tl.state(s0.shape, layer, "a"), _layer_spec(w["nw"].shape, layer),
                   tl.rows(D_MODEL, "b"), _layer_spec(w["wout"].shape, layer, single=True),
                   _layer_spec(w["gmix"].shape, layer), _layer_spec(f2[0].shape, layer),
                   _layer_spec(f2[1].shape, layer, single=True), _layer_spec(f2[2].shape, layer, single=True),
                   _layer_spec(f2[3].shape, layer)])
    if gfin is not None:
        args.append(gfin)
        in_specs.append(pl.BlockSpec(gfin.shape, lambda g: (0, 0)))
    aliases = {}
    if prev_h is not None:
        aliases = {len(args): 1, len(args) + 1: 2}
        args += [prev_h, prev_s]
        in_specs += [pl.BlockSpec(memory_space=pl.ANY)] * 2
    return pl.pallas_call(
        functools.partial(_kernel_b, tl=tl, aliased=prev_h is not None, final=gfin is not None),
        grid=(tl.n + 1,),
        in_specs=in_specs,
        out_specs=[tl.rows(D_MODEL, "b"), tl.state(h0.shape, layer, "a"), tl.state(s0.shape, layer, "a")],
        out_shape=[jax.ShapeDtypeStruct((bn, l, D_MODEL), F32), jax.ShapeDtypeStruct(h0.shape, F32),
                   jax.ShapeDtypeStruct(s0.shape, F32)],
        scratch_shapes=[pltpu.VMEM((tl.r, D_FF), BF16), pltpu.VMEM((2, ns, t, D_MODEL), F32)],
        input_output_aliases=aliases,
        compiler_params=_cparams(),
        name="recur_ffn2",
    )(*args)


def _lane_pad(mat, offset):
    depth, n = mat.shape
    out = jnp.zeros((depth, 1, LANES), F32)
    return lax.dynamic_update_slice(out, mat.reshape(depth, 1, n).astype(F32), (0, 0, offset))


def _weights(ffn1_norm_pre, ffn1_w_up, ffn1_w_down, ffn1_norm_post, mix_norm_pre, w_in, conv_w, conv_b_rg,
             rg_w_a, rg_b_a, rg_w_x, rg_b_x, rg_lambda, dn_a_log, dn_dt_bias, dn_norm_w, w_out, mix_norm_post,
             ffn2_norm_pre, ffn2_w_up, ffn2_w_down, ffn2_norm_post):
    depth = w_in.shape[0]
    row = lambda v: v.reshape(depth, 1, -1)
    wscal = jnp.zeros((depth, D_MODEL, LANES), BF16).at[:, :, :2 * DN_HEADS].set(
        w_in[:, :, MAIN_COLS:].astype(BF16))
    eye = jnp.eye(RG_BLOCKS, dtype=F32)
    bd = lambda w: jnp.einsum('lhij,hg->lhigj', w, eye).reshape(depth, RG_WIDTH, RG_WIDTH)
    return dict(
        f1=(row(ffn1_norm_pre), ffn1_w_up.astype(BF16), ffn1_w_down.astype(BF16), row(ffn1_norm_post)),
        f2=(row(ffn2_norm_pre), ffn2_w_up.astype(BF16), ffn2_w_down.astype(BF16), row(ffn2_norm_post)),
        prep=dict(gpre=row(mix_norm_pre), wmain=w_in[:, :, :MAIN_COLS].astype(BF16), wscal=wscal, cw=conv_w,
                  cb=row(conv_b_rg), wrg=jnp.concatenate([bd(rg_w_a), bd(rg_w_x)], axis=2).astype(BF16),
                  brg=jnp.concatenate([row(rg_b_a), row(rg_b_x)], axis=2), lam=row(rg_lambda),
                  alog=_lane_pad(dn_a_log, DN_HEADS), dtb=_lane_pad(dn_dt_bias, DN_HEADS)),
        nw=row(dn_norm_w),
        wout=w_out.astype(BF16),
        gmix=row(mix_norm_post),
    )


def _run(x, conv, rg, dn, w, final_norm, *, blk_a, blk_b):
    depth = conv.shape[0]
    new_conv = new_rg = new_dn = None
    for li in range(depth):
        x1, *prep_out, new_conv = _call_a(x, li, conv, new_conv, w, ns=blk_a[0], t=blk_a[1])
        gfin = final_norm.reshape(1, -1) if li == depth - 1 else None
        x, new_rg, new_dn = _call_b(prep_out, x1, li, rg, dn, new_rg, new_dn, w, gfin, ns=blk_b[0], t=blk_b[1])
    return x, new_conv, new_rg, new_dn


def kernel(x_prompt, x_sample, state_conv, state_rglru, state_delta, ffn1_norm_pre, ffn1_w_up, ffn1_w_down, ffn1_norm_post, mix_norm_pre, w_in, conv_w, conv_b_rg, rg_w_a, rg_b_a, rg_w_x, rg_b_x, rg_lambda, dn_a_log, dn_dt_bias, dn_norm_w, w_out, mix_norm_post, ffn2_norm_pre, ffn2_w_up, ffn2_w_down, ffn2_norm_post, final_norm):
    depth = w_in.shape[0]
    w = _weights(ffn1_norm_pre, ffn1_w_up, ffn1_w_down, ffn1_norm_post, mix_norm_pre, w_in, conv_w, conv_b_rg,
                 rg_w_a, rg_b_a, rg_w_x, rg_b_x, rg_lambda, dn_a_log, dn_dt_bias, dn_norm_w, w_out,
                 mix_norm_post, ffn2_norm_pre, ffn2_w_up, ffn2_w_down, ffn2_norm_post)
    bp, lp, _ = x_prompt.shape
    bs, ls, _ = x_sample.shape
    dt = x_prompt.dtype
    zero_conv = jnp.zeros((depth, bp, CONV_WIDTH - 1, CONV_CH), dt)
    zero_rg = jnp.zeros((depth, bp, 1, RG_WIDTH), dt)
    zero_dn = jnp.zeros((depth, bp, DN_HEADS, DN_HEAD_DIM, DN_HEAD_DIM), dt)
    y_p, conv_p, rg_p, dn_p = _run(x_prompt, zero_conv, zero_rg, zero_dn, w, final_norm,
                                   blk_a=(4, CHUNK), blk_b=(4, CHUNK))
    y_s, conv_s, rg_s, dn_s = _run(x_sample, state_conv, state_rglru.reshape(depth, bs, 1, RG_WIDTH),
                                   state_delta, w, final_norm,
                                   blk_a=(4 * CHUNK // ls, ls), blk_b=(2 * CHUNK // ls, ls))
    return (y_p, y_s, conv_p, rg_p.reshape(depth, bp, RG_WIDTH), dn_p,
            conv_s, rg_s.reshape(depth, bs, RG_WIDTH), dn_s)
```

```python
import functools

import jax
import jax.numpy as jnp
from jax import lax
from jax.experimental import pallas as pl
from jax.experimental.pallas import tpu as pltpu

F32 = jnp.float32
BF16 = jnp.bfloat16

D_MODEL = 1024
D_FF = 2816
RG_WIDTH = 512
RG_BLOCKS = 8
RG_BLOCK = RG_WIDTH // RG_BLOCKS
RG_C = 8.0
DN_HEADS = 4
DN_HEAD_DIM = 128
DN_WIDTH = DN_HEADS * DN_HEAD_DIM
CONV_WIDTH = 4
CONV_CH = RG_WIDTH + 3 * DN_WIDTH
GATE_COLS = RG_WIDTH + DN_WIDTH
MAIN_COLS = CONV_CH + GATE_COLS
EPS = 1e-6

LANES = 128
SUBLANES = 8
CHUNK = 64
CONV_PAD = SUBLANES
FF_TILE = 256
TM = 512
VMEM_LIMIT = 56 * 1024 * 1024
PREP_WIDTHS = (RG_WIDTH, RG_WIDTH, RG_WIDTH, DN_WIDTH, DN_WIDTH, DN_WIDTH, DN_WIDTH, LANES)
PREP_COLS = sum(PREP_WIDTHS)


def _prep_views(prep_ref):
    views, off = [], 0
    for width in PREP_WIDTHS:
        views.append(prep_ref.at[:, :, off:off + width])
        off += width
    return tuple(views)


def _rms(x, g):
    return x * lax.rsqrt(jnp.mean(x * x, axis=-1, keepdims=True) + EPS) * g


def _silu(x):
    return x * jax.nn.sigmoid(x)


def _softplus(x):
    return jnp.maximum(x, 0.0) + jnp.log1p(jnp.exp(-jnp.abs(x)))


def _bdot(a, b):
    return jnp.dot(a.astype(BF16), b.astype(BF16), preferred_element_type=F32)


def _bdot_nt(a, b):
    return lax.dot_general(a.astype(BF16), b.astype(BF16), (((1,), (1,)), ((), ())),
                           preferred_element_type=F32)


def _fdot(a, b):
    return jnp.dot(a, b, preferred_element_type=F32, precision=lax.Precision.HIGHEST)


def _fdot_nt(a, b):
    return lax.dot_general(a, b, (((1,), (1,)), ((), ())), preferred_element_type=F32,
                           precision=lax.Precision.HIGHEST)


def _layer_spec(shape, layer, *, single=False):
    nd = len(shape)
    kw = dict(pipeline_mode=pl.Buffered(1)) if single else {}
    return pl.BlockSpec((None,) + tuple(shape[1:]), lambda *_: (layer,) + (0,) * (nd - 1), **kw)


class _Tiling:
    def __init__(self, bn, l, ns, t):
        assert bn % ns == 0 and l % t == 0
        self.ns, self.t, self.r = ns, t, ns * t
        self.nt = l // t
        self.n = (bn // ns) * self.nt

    def tile_a(self, g):
        return jnp.minimum(g, self.n - 1)

    def tile_b(self, g):
        return jnp.maximum(g - 1, 0)

    def rows(self, width, stage):
        tile = self.tile_a if stage == "a" else self.tile_b
        return pl.BlockSpec((self.ns, self.t, width),
                            lambda g: (tile(g) // self.nt, tile(g) % self.nt, 0))

    def state(self, shape, layer, stage):
        tile = self.tile_a if stage == "a" else self.tile_b
        nd = len(shape)
        return pl.BlockSpec((None, self.ns) + tuple(shape[2:]),
                            lambda g: (layer, tile(g) // self.nt) + (0,) * (nd - 2))


def _cparams():
    return pltpu.CompilerParams(dimension_semantics=("arbitrary",), vmem_limit_bytes=VMEM_LIMIT)


def _interleave(*stages):
    live = list(stages)
    while live:
        for st in list(live):
            try:
                next(st)
            except StopIteration:
                live.remove(st)


def _ffn_steps(x_fn, gpre_ref, wup_ref, wdn_ref, gpost_ref, hid_ref, finish, h_fn=None):
    x = x_fn()
    h = _rms(x, gpre_ref[...]).astype(BF16) if h_fn is None else h_fn()
    yield
    for j in range(D_FF // FF_TILE):
        g = jnp.dot(h, wup_ref[:, j * FF_TILE:(j + 1) * FF_TILE], preferred_element_type=F32)
        u = jnp.dot(h, wup_ref[:, D_FF + j * FF_TILE:D_FF + (j + 1) * FF_TILE], preferred_element_type=F32)
        hid_ref[:, j * FF_TILE:(j + 1) * FF_TILE] = (_silu(g) * u).astype(BF16)
        yield
    ys = []
    for c in range(D_MODEL // FF_TILE):
        ys.append(jnp.dot(hid_ref[...], wdn_ref[:, c * FF_TILE:(c + 1) * FF_TILE], preferred_element_type=F32))
        yield
    finish(x + 0.5 * _rms(jnp.concatenate(ys, axis=1), gpost_ref[...]))


def _prep_steps(x_fn, w, outs, ncst_ref, cscr_ref, ns, t):
    gpre_ref, wmain_ref, wscal_ref, cw_ref, cb_ref, wrg_ref, brg_ref, lam_ref, alog_ref, dtb_ref = w
    a_ref, b_ref, rgate_ref, q_ref, k_ref, v_ref, zg_ref, scal_ref = outs
    r = ns * t
    cw_n = FF_TILE
    h = _rms(x_fn(), gpre_ref[...]).astype(BF16)
    yield

    def proj_cols(c0):
        return jnp.dot(h, wmain_ref[:, c0:c0 + cw_n], preferred_element_type=F32)

    def conv_cols(c0):
        cols = slice(c0, c0 + cw_n)
        cscr_ref[:, CONV_PAD:CONV_PAD + t, cols] = proj_cols(c0).reshape(ns, t, cw_n)
        conv = None
        for j in range(CONV_WIDTH):
            lo = CONV_PAD - (CONV_WIDTH - 1) + j
            term = cscr_ref[:, lo:lo + t, cols] * cw_ref[j:j + 1, cols]
            conv = term if conv is None else conv + term
        hist = cscr_ref[:, CONV_PAD + t - (CONV_WIDTH - 1):CONV_PAD + t, cols]
        ncst_ref[:, :, cols] = hist
        cscr_ref[:, CONV_PAD - (CONV_WIDTH - 1):CONV_PAD, cols] = hist
        return conv.reshape(r, cw_n)

    xr_parts = []
    for c0 in range(0, RG_WIDTH, cw_n):
        xr_parts.append(conv_cols(c0) + cb_ref[:, c0:c0 + cw_n])
        yield
    xr = jnp.concatenate(xr_parts, axis=1)
    gax = jnp.dot(xr.astype(BF16), wrg_ref[...], preferred_element_type=F32) + brg_ref[...]
    rr = jax.nn.sigmoid(gax[:, :RG_WIDTH])
    ii = jax.nn.sigmoid(gax[:, RG_WIDTH:])
    log_a = -RG_C * rr * _softplus(-lam_ref[...])
    a = jnp.exp(log_a)
    a_ref[...] = a.reshape(ns, t, RG_WIDTH)
    one_minus_a2 = -jnp.tanh(log_a) * (a * a + 1.0)
    b_ref[...] = (jnp.sqrt(one_minus_a2) * (ii * xr)).reshape(ns, t, RG_WIDTH)
    yield

    for base, ref, scale in ((RG_WIDTH, q_ref, DN_HEAD_DIM ** -0.5), (RG_WIDTH + DN_WIDTH, k_ref, None)):
        for c0 in range(0, DN_WIDTH, cw_n):
            act = _silu(conv_cols(base + c0))
            for hh in range(cw_n // DN_HEAD_DIM):
                xh = act[:, hh * DN_HEAD_DIM:(hh + 1) * DN_HEAD_DIM]
                xn = xh * lax.rsqrt(jnp.sum(xh * xh, axis=-1, keepdims=True) + EPS)
                if scale is not None:
                    xn = xn * scale
                lo = c0 + hh * DN_HEAD_DIM
                ref[:, :, lo:lo + DN_HEAD_DIM] = xn.reshape(ns, t, DN_HEAD_DIM)
            yield
    for c0 in range(0, DN_WIDTH, cw_n):
        v_ref[:, :, c0:c0 + cw_n] = _silu(conv_cols(RG_WIDTH + 2 * DN_WIDTH + c0)).reshape(ns, t, cw_n)
        yield

    for c0 in range(0, RG_WIDTH, cw_n):
        rgate_ref[:, :, c0:c0 + cw_n] = jax.nn.gelu(proj_cols(CONV_CH + c0)).reshape(ns, t, cw_n)
        yield
    for c0 in range(0, DN_WIDTH, cw_n):
        zg_ref[:, :, c0:c0 + cw_n] = _silu(proj_cols(CONV_CH + RG_WIDTH + c0)).reshape(ns, t, cw_n)
        yield

    scal = jnp.dot(h, wscal_ref[...], preferred_element_type=F32)
    lane = lax.broadcasted_iota(jnp.int32, scal.shape, 1)
    beta = jax.nn.sigmoid(scal)
    g = -jnp.exp(alog_ref[...]) * _softplus(scal + dtb_ref[...])
    scal_ref[...] = jnp.where(lane < DN_HEADS, beta, g).reshape(ns, t, LANES)


def _recur_steps(ins, nw_ref, hout_ref, sout_ref, mix_ref, ns, t):
    a_ref, b_ref, rgate_ref, q_ref, k_ref, v_ref, zg_ref, scal_ref = ins
    r = ns * t
    seg = min(CHUNK, t)
    nseg = CHUNK // seg
    nchunk = r // CHUNK

    rowmod = lax.broadcasted_iota(jnp.int32, (t // SUBLANES, SUBLANES, RG_WIDTH), 1)

    def rg_seq(si, carry):
        a = a_ref[si].reshape(t // SUBLANES, SUBLANES, RG_WIDTH)
        b = b_ref[si].reshape(t // SUBLANES, SUBLANES, RG_WIDTH)
        for s in (1, 2, 4):
            m = rowmod >= s
            a_sh = pltpu.roll(a, s, axis=1)
            b_sh = pltpu.roll(b, s, axis=1)
            b = jnp.where(m, a * b_sh + b, b)
            a = jnp.where(m, a * a_sh, a)
        hc = hout_ref[si]
        for gi in range(t // SUBLANES):
            rows = slice(gi * SUBLANES, (gi + 1) * SUBLANES)
            hb = a[gi] * hc + b[gi]
            mix_ref[si, rows, :RG_WIDTH] = (hb * rgate_ref[si, rows, :]).astype(mix_ref.dtype)
            hc = hb[SUBLANES - 1:SUBLANES, :]
        hout_ref[si] = hc
        return carry

    lax.fori_loop(0, ns, rg_seq, 0)
    yield

    ri = lax.broadcasted_iota(jnp.int32, (CHUNK, CHUNK), 0)
    ci = lax.broadcasted_iota(jnp.int32, (CHUNK, CHUNK), 1)
    seg_shift = seg.bit_length() - 1
    same = (ri >> seg_shift) == (ci >> seg_shift)
    incl = jnp.logical_and(same, ri >= ci)
    strict = jnp.logical_and(same, ri > ci)
    incl_f = incl.astype(F32)
    eye = (ri == ci).astype(F32)
    sel = (lax.broadcasted_iota(jnp.int32, (SUBLANES, LANES), 0)
           == lax.broadcasted_iota(jnp.int32, (SUBLANES, LANES), 1)).astype(F32)
    nw = nw_ref[...]

    def chunk_index(c):
        if t >= CHUNK:
            per_seq = t // CHUNK
            return c // per_seq, slice((c % per_seq) * CHUNK, (c % per_seq + 1) * CHUNK)
        per_chunk = CHUNK // t
        return slice(c * per_chunk, (c + 1) * per_chunk), slice(None)

    def load_rows(ref, c, lanes):
        i0, i1 = chunk_index(c)
        return ref[i0, i1, lanes].reshape(CHUNK, -1)

    def store_rows(ref, c, lanes, val):
        i0, i1 = chunk_index(c)
        ref[i0, i1, lanes] = val if t >= CHUNK else val.reshape(CHUNK // t, t, -1)

    def state_index(c, jj):
        return c // (t // CHUNK) if t >= CHUNK else c * nseg + jj

    chains = []
    for c in range(nchunk):
        sc = load_rows(scal_ref, c, slice(None))
        gall = _fdot(incl_f, sc)
        gt = _fdot_nt(sel, gall)
        for hh in range(DN_HEADS):
            sl = slice(hh * DN_HEAD_DIM, (hh + 1) * DN_HEAD_DIM)
            chains.append(dict(
                c=c, hh=hh, sl=sl, beta=sc[:, hh:hh + 1],
                gc=gall[:, DN_HEADS + hh:DN_HEADS + hh + 1],
                gr=gt[DN_HEADS + hh:DN_HEADS + hh + 1, :]))
    yield
    for ch in chains:
        qh = load_rows(q_ref, ch["c"], ch["sl"])
        kh = load_rows(k_ref, ch["c"], ch["sl"])
        dec = jnp.where(incl, jnp.exp(jnp.where(incl, ch["gc"] - ch["gr"], 0.0)), 0.0)
        kb = kh * ch["beta"]
        kq = _bdot_nt(jnp.concatenate([kb, qh], axis=0), kh)
        ch.update(qh=qh, kh=kh, kb=kb,
                  x=jnp.where(strict, kq[:CHUNK] * dec, 0.0),
                  att=kq[CHUNK:] * dec)
    for ch in chains:
        ch["p"] = eye - ch["x"]
        if seg > 2:
            ch["x"] = _bdot(ch["x"], ch["x"])
    yield
    span = 4
    while span < seg:
        for ch in chains:
            both = _bdot(jnp.concatenate([ch["x"], ch["p"]], axis=0), ch["x"])
            ch["x"], ch["p"] = both[:CHUNK], ch["p"] + both[CHUNK:]
        yield
        span *= 2
    if seg > 2:
        for ch in chains:
            ch["p"] = ch["p"] + _bdot(ch["p"], ch["x"])
        yield
    for ch in chains:
        vh = load_rows(v_ref, ch["c"], ch["sl"])
        eg = jnp.exp(ch["gc"])
        sol = _bdot(ch["p"], jnp.concatenate([vh * ch["beta"], ch["kb"] * eg], axis=1))
        ch.update(u=sol[:, :DN_HEAD_DIM], w=sol[:, DN_HEAD_DIM:], qe=ch["qh"] * eg)
    yield
    for ch in chains:
        ch["s_old"], ch["rr"] = [], []
        for jj in range(nseg):
            rs = slice(jj * seg, (jj + 1) * seg)
            s_old = sout_ref[state_index(ch["c"], jj), ch["hh"]]
            ch["s_old"].append(s_old)
            ch["rr"].append(_bdot(jnp.concatenate([ch["w"][rs], ch["qe"][rs]], axis=0), s_old))
    yield
    for ch in chains:
        vnews = []
        for jj in range(nseg):
            rs = slice(jj * seg, (jj + 1) * seg)
            vnew = ch["u"][rs] - ch["rr"][jj][:seg]
            gl = ch["gc"][(jj + 1) * seg - 1:(jj + 1) * seg, :]
            kdec = ch["kh"][rs] * jnp.exp(gl - ch["gc"][rs])
            if nseg == 1:
                both = _bdot(jnp.concatenate([ch["att"], kdec.T], axis=0), vnew)
                ch["ov"], upd = both[:CHUNK], both[CHUNK:]
            else:
                upd = _bdot(kdec.T, vnew)
            sout_ref[state_index(ch["c"], jj), ch["hh"]] = ch["s_old"][jj] * jnp.exp(gl) + upd
            vnews.append(vnew)
        if nseg > 1:
            ch["ov"] = _bdot(ch["att"], jnp.concatenate(vnews, axis=0))
    yield
    for ch in chains:
        oint = ch["rr"][0][seg:] if nseg == 1 else jnp.concatenate([x[seg:] for x in ch["rr"]], axis=0)
        o = oint + ch["ov"]
        o = _rms(o, nw) * load_rows(zg_ref, ch["c"], ch["sl"])
        store_rows(mix_ref, ch["c"], slice(RG_WIDTH + ch["hh"] * DN_HEAD_DIM,
                                           RG_WIDTH + (ch["hh"] + 1) * DN_HEAD_DIM), o.astype(mix_ref.dtype))


def _kernel_a(x_ref, gpre1_ref, wup_ref, wdn_ref, gpost1_ref,
              gprem_ref, wmain_ref, wscal_ref, cw_ref, cb_ref, wrg_ref, brg_ref, lam_ref, alog_ref, dtb_ref,
              cst_ref, *rest, tl, aliased):
    if aliased:
        rest = rest[1:]
    x1_ref, prep_ref, ncst_ref, hid_ref, xs_ref, cscr_ref = rest
    outs = _prep_views(prep_ref)
    g = pl.program_id(0)
    hist_rows = slice(CONV_PAD - (CONV_WIDTH - 1), CONV_PAD)

    @pl.when(g == 0)
    def _():
        xs_ref[...] = jnp.zeros_like(xs_ref)
        cscr_ref[...] = jnp.zeros_like(cscr_ref)

    @pl.when((g - 1) % tl.nt == 0)
    def _():
        cscr_ref[:, hist_rows, :] = cst_ref[...]

    def finish(x1):
        x1_ref[...] = x1.reshape(tl.ns, tl.t, D_MODEL)
        xs_ref[g % 2] = x1

    w = (gprem_ref, wmain_ref, wscal_ref, cw_ref, cb_ref, wrg_ref, brg_ref, lam_ref, alog_ref, dtb_ref)
    _interleave(
        _prep_steps(lambda: xs_ref[(g + 1) % 2], w, outs, ncst_ref, cscr_ref, tl.ns, tl.t),
        _ffn_steps(lambda: x_ref[...].reshape(tl.r, D_MODEL), gpre1_ref, wup_ref, wdn_ref, gpost1_ref, hid_ref,
                   finish))


def _call_a(x, layer, conv_state, prev_conv, w, *, ns, t):
    bn, l, _ = x.shape
    tl = _Tiling(bn, l, ns, t)
    f1, pw = w["f1"], w["prep"]
    big = {"wup", "wdn", "wmain", "wrg"}
    names = ["gpre", "wmain", "wscal", "cw", "cb", "wrg", "brg", "lam", "alog", "dtb"]
    args = [x, *f1] + [pw[n] for n in names] + [conv_state]
    in_specs = ([tl.rows(D_MODEL, "a"), _layer_spec(f1[0].shape, layer), _layer_spec(f1[1].shape, layer, single=True),
                 _layer_spec(f1[2].shape, layer, single=True), _layer_spec(f1[3].shape, layer)]
                + [_layer_spec(pw[n].shape, layer, single=n in big) for n in names]
                + [tl.state(conv_state.shape, layer, "b")])
    in_specs[5 + names.index("wmain")] = pl.BlockSpec(
        (None, D_MODEL, MAIN_COLS), lambda g: (layer, 0, 0), pipeline_mode=pl.Buffered(1))
    aliases = {}
    if prev_conv is not None:
        aliases = {len(args): 2}
        args.append(prev_conv)
        in_specs.append(pl.BlockSpec(memory_space=pl.ANY))
    return pl.pallas_call(
        functools.partial(_kernel_a, tl=tl, aliased=prev_conv is not None),
        grid=(tl.n + 1,),
        in_specs=in_specs,
        out_specs=[tl.rows(D_MODEL, "a"), tl.rows(PREP_COLS, "b"), tl.state(conv_state.shape, layer, "b")],
        out_shape=[jax.ShapeDtypeStruct((bn, l, D_MODEL), F32), jax.ShapeDtypeStruct((bn, l, PREP_COLS), F32),
                   jax.ShapeDtypeStruct(conv_state.shape, F32)],
        scratch_shapes=[pltpu.VMEM((tl.r, D_FF), BF16), pltpu.VMEM((2, tl.r, D_MODEL), F32),
                        pltpu.VMEM((ns, t + CONV_PAD, CONV_CH), F32)],
        input_output_aliases=aliases,
        compiler_params=_cparams(),
        name="ffn1_prep",
    )(*args)


def _kernel_b(prep_ref, h0_ref, s0_ref, nw_ref, *rest, tl, aliased):
    if aliased:
        rest = rest[2:]
    mix_ref, hout_ref, sout_ref = rest
    g = pl.program_id(0)

    @pl.when(g % tl.nt == 0)
    def _():
        hout_ref[...] = h0_ref[...]
        sout_ref[...] = s0_ref[...]

    _interleave(_recur_steps(_prep_views(prep_ref), nw_ref, hout_ref, sout_ref, mix_ref, tl.ns, tl.t))


def _call_b(prep, layer, h0, s0, prev_h, prev_s, nw, *, ns, t):
    bn, l, _ = prep.shape
    tl = _Tiling(bn, l, ns, t)
    args = [prep, h0, s0, nw]
    in_specs = [tl.rows(PREP_COLS, "a"), tl.state(h0.shape, layer, "a"), tl.state(s0.shape, layer, "a"),
                _layer_spec(nw.shape, layer)]
    aliases = {}
    if prev_h is not None:
        aliases = {len(args): 1, len(args) + 1: 2}
        args += [prev_h, prev_s]
        in_specs += [pl.BlockSpec(memory_space=pl.ANY)] * 2
    return pl.pallas_call(
        functools.partial(_kernel_b, tl=tl, aliased=prev_h is not None),
        grid=(tl.n,),
        in_specs=in_specs,
        out_specs=[tl.rows(D_MODEL, "a"), tl.state(h0.shape, layer, "a"), tl.state(s0.shape, layer, "a")],
        out_shape=[jax.ShapeDtypeStruct((bn, l, D_MODEL), BF16), jax.ShapeDtypeStruct(h0.shape, F32),
                   jax.ShapeDtypeStruct(s0.shape, F32)],
        input_output_aliases=aliases,
        compiler_params=_cparams(),
        name="recur",
    )(*args)


def _kernel_c(xp_ref, xs_ref, mp_ref, ms_ref, wout_ref, gmix_ref, gpre_ref, wup_ref, wdn_ref, gpost_ref,
              *rest, nbp, final):
    rest = list(rest)
    gfin_ref = rest.pop(0) if final else None
    yp_ref, ys_ref, hid_ref, xmid_ref, h_ref = rest
    i = pl.program_id(0)
    head_is_prompt = i < nbp
    body_is_prompt = i - 1 < nbp

    @pl.when(i == 0)
    def _():
        xmid_ref[...] = jnp.zeros_like(xmid_ref)
        h_ref[...] = jnp.zeros_like(h_ref)

    def head_steps():
        for _ in range(D_FF // FF_TILE):
            yield
        x = jnp.where(head_is_prompt, xp_ref[...], xs_ref[...])
        m = jnp.where(head_is_prompt, mp_ref[...], ms_ref[...])
        y = jnp.dot(m, wout_ref[...], preferred_element_type=F32)
        yield
        xm = x + _rms(y, gmix_ref[...])
        xmid_ref[i % 2] = xm
        yield
        h_ref[i % 2] = _rms(xm, gpre_ref[...]).astype(BF16)

    def finish(out):
        if final:
            out = _rms(out, gfin_ref[...])

        @pl.when(body_is_prompt)
        def _():
            yp_ref[...] = out

        @pl.when(jnp.logical_not(body_is_prompt))
        def _():
            ys_ref[...] = out

    prev = (i + 1) % 2
    _interleave(
        _ffn_steps(lambda: xmid_ref[prev], gpre_ref, wup_ref, wdn_ref, gpost_ref, hid_ref, finish,
                   h_fn=lambda: h_ref[prev]),
        head_steps())


def _call_c(x1_p, x1_s, mix_p, mix_s, layer, w, gfin):
    n_p, n_s = x1_p.shape[0], x1_s.shape[0]
    nbp, nbs = n_p // TM, n_s // TM
    f2 = w["f2"]

    def specs(tile):
        return (pl.BlockSpec((TM, D_MODEL), lambda i: (jnp.clip(tile(i), 0, nbp - 1), 0)),
                pl.BlockSpec((TM, D_MODEL), lambda i: (jnp.clip(tile(i) - nbp, 0, nbs - 1), 0)))

    p_in, s_in = specs(lambda i: i)
    p_out, s_out = specs(lambda i: i - 1)
    args = [x1_p, x1_s, mix_p, mix_s, w["wout"], w["gmix"], *f2]
    in_specs = [p_in, s_in, p_in, s_in, _layer_spec(w["wout"].shape, layer, single=True),
                _layer_spec(w["gmix"].shape, layer), _layer_spec(f2[0].shape, layer),
                _layer_spec(f2[1].shape, layer, single=True), _layer_spec(f2[2].shape, layer, single=True),
                _layer_spec(f2[3].shape, layer)]
    if gfin is not None:
        args.append(gfin)
        in_specs.append(pl.BlockSpec(gfin.shape, lambda i: (0, 0)))
    return pl.pallas_call(
        functools.partial(_kernel_c, nbp=nbp, final=gfin is not None),
        grid=(nbp + nbs + 1,),
        in_specs=in_specs,
        out_specs=[p_out, s_out],
        out_shape=[jax.ShapeDtypeStruct((n_p, D_MODEL), F32), jax.ShapeDtypeStruct((n_s, D_MODEL), F32)],
        scratch_shapes=[pltpu.VMEM((TM, D_FF), BF16), pltpu.VMEM((2, TM, D_MODEL), F32),
                        pltpu.VMEM((2, TM, D_MODEL), BF16)],
        compiler_params=_cparams(),
        name="ffn2",
    )(*args)


def _lane_pad(mat, offset):
    depth, n = mat.shape
    out = jnp.zeros((depth, 1, LANES), F32)
    return lax.dynamic_update_slice(out, mat.reshape(depth, 1, n).astype(F32), (0, 0, offset))


def _weights(ffn1_norm_pre, ffn1_w_up, ffn1_w_down, ffn1_norm_post, mix_norm_pre, w_in, conv_w, conv_b_rg,
             rg_w_a, rg_b_a, rg_w_x, rg_b_x, rg_lambda, dn_a_log, dn_dt_bias, dn_norm_w, w_out, mix_norm_post,
             ffn2_norm_pre, ffn2_w_up, ffn2_w_down, ffn2_norm_post):
    depth = w_in.shape[0]
    row = lambda v: v.reshape(depth, 1, -1)
    wscal = jnp.zeros((depth, D_MODEL, LANES), BF16).at[:, :, :2 * DN_HEADS].set(
        w_in[:, :, MAIN_COLS:].astype(BF16))
    eye = jnp.eye(RG_BLOCKS, dtype=F32)
    bd = lambda w: jnp.einsum('lhij,hg->lhigj', w, eye).reshape(depth, RG_WIDTH, RG_WIDTH)
    return dict(
        f1=(row(ffn1_norm_pre), ffn1_w_up.astype(BF16), ffn1_w_down.astype(BF16), row(ffn1_norm_post)),
        f2=(row(ffn2_norm_pre), ffn2_w_up.astype(BF16), ffn2_w_down.astype(BF16), row(ffn2_norm_post)),
        prep=dict(gpre=row(mix_norm_pre), wmain=w_in.astype(BF16), wscal=wscal, cw=conv_w,
                  cb=row(conv_b_rg), wrg=jnp.concatenate([bd(rg_w_a), bd(rg_w_x)], axis=2).astype(BF16),
                  brg=jnp.concatenate([row(rg_b_a), row(rg_b_x)], axis=2), lam=row(rg_lambda),
                  alog=_lane_pad(dn_a_log, DN_HEADS), dtb=_lane_pad(dn_dt_bias, DN_HEADS)),
        nw=row(dn_norm_w),
        wout=w_out.astype(BF16),
        gmix=row(mix_norm_post),
    )


def kernel(x_prompt, x_sample, state_conv, state_rglru, state_delta, ffn1_norm_pre, ffn1_w_up, ffn1_w_down, ffn1_norm_post, mix_norm_pre, w_in, conv_w, conv_b_rg, rg_w_a, rg_b_a, rg_w_x, rg_b_x, rg_lambda, dn_a_log, dn_dt_bias, dn_norm_w, w_out, mix_norm_post, ffn2_norm_pre, ffn2_w_up, ffn2_w_down, ffn2_norm_post, final_norm):
    depth = w_in.shape[0]
    w = _weights(ffn1_norm_pre, ffn1_w_up, ffn1_w_down, ffn1_norm_post, mix_norm_pre, w_in, conv_w, conv_b_rg,
                 rg_w_a, rg_b_a, rg_w_x, rg_b_x, rg_lambda, dn_a_log, dn_dt_bias, dn_norm_w, w_out,
                 mix_norm_post, ffn2_norm_pre, ffn2_w_up, ffn2_w_down, ffn2_norm_post)
    bp, lp, _ = x_prompt.shape
    bs, ls, _ = x_sample.shape
    dt = x_prompt.dtype
    pops = [
        dict(x=x_prompt, blk_a=(4, CHUNK), blk_b=(bp, CHUNK),
             conv=jnp.zeros((depth, bp, CONV_WIDTH - 1, CONV_CH), dt),
             rg=jnp.zeros((depth, bp, 1, RG_WIDTH), dt),
             dn=jnp.zeros((depth, bp, DN_HEADS, DN_HEAD_DIM, DN_HEAD_DIM), dt)),
        dict(x=x_sample, blk_a=(4 * CHUNK // ls, ls), blk_b=(2 * CHUNK // ls, ls),
             conv=state_conv, rg=state_rglru.reshape(depth, bs, 1, RG_WIDTH), dn=state_delta),
    ]
    for p in pops:
        p.update(new_conv=None, new_rg=None, new_dn=None)
    pp, ps = pops
    flat = lambda v: v.reshape(-1, D_MODEL)
    for li in range(depth):
        for p in pops:
            p["x1"], prep, p["new_conv"] = _call_a(p["x"], li, p["conv"], p["new_conv"], w,
                                                   ns=p["blk_a"][0], t=p["blk_a"][1])
            p["mix"], p["new_rg"], p["new_dn"] = _call_b(prep, li, p["rg"], p["dn"], p["new_rg"], p["new_dn"],
                                                         w["nw"], ns=p["blk_b"][0], t=p["blk_b"][1])
        gfin = final_norm.reshape(1, -1) if li == depth - 1 else None
        y_p, y_s = _call_c(flat(pp["x1"]), flat(ps["x1"]), flat(pp["mix"]), flat(ps["mix"]), li, w, gfin)
        pp["x"], ps["x"] = y_p.reshape(pp["x"].shape), y_s.reshape(ps["x"].shape)
    return (pp["x"], ps["x"], pp["new_conv"], pp["new_rg"].reshape(depth, bp, RG_WIDTH), pp["new_dn"],
            ps["new_conv"], ps["new_rg"].reshape(depth, bs, RG_WIDTH), ps["new_dn"])
```

```python
import functools

import jax
import jax.numpy as jnp
from jax import lax
from jax.experimental import pallas as pl
from jax.experimental.pallas import tpu as pltpu

F32 = jnp.float32
BF16 = jnp.bfloat16

D_MODEL = 1024
D_FF = 2816
RG_WIDTH = 512
RG_BLOCKS = 8
RG_BLOCK = RG_WIDTH // RG_BLOCKS
RG_C = 8.0
DN_HEADS = 4
DN_HEAD_DIM = 128
DN_WIDTH = DN_HEADS * DN_HEAD_DIM
CONV_WIDTH = 4
CONV_CH = RG_WIDTH + 3 * DN_WIDTH
GATE_COLS = RG_WIDTH + DN_WIDTH
MAIN_COLS = CONV_CH + GATE_COLS
EPS = 1e-6

LANES = 128
SUBLANES = 8
CHUNK = 64
CONV_PAD = SUBLANES
FF_TILE = 256
TM = 512
VMEM_LIMIT = 56 * 1024 * 1024
PREP_WIDTHS = (RG_WIDTH, RG_WIDTH, RG_WIDTH, DN_WIDTH, DN_WIDTH, DN_WIDTH, DN_WIDTH, LANES)
PREP_COLS = sum(PREP_WIDTHS)


def _prep_views(prep_ref):
    views, off = [], 0
    for width in PREP_WIDTHS:
        views.append(prep_ref.at[:, :, off:off + width])
        off += width
    return tuple(views)


def _rms(x, g):
    return x * lax.rsqrt(jnp.mean(x * x, axis=-1, keepdims=True) + EPS) * g


def _silu(x):
    return x * jax.nn.sigmoid(x)


def _softplus(x):
    return jnp.maximum(x, 0.0) + jnp.log1p(jnp.exp(-jnp.abs(x)))


def _bdot(a, b):
    return jnp.dot(a.astype(BF16), b.astype(BF16), preferred_element_type=F32)


def _bdot_nt(a, b):
    return lax.dot_general(a.astype(BF16), b.astype(BF16), (((1,), (1,)), ((), ())),
                           preferred_element_type=F32)


def _fdot(a, b):
    return jnp.dot(a, b, preferred_element_type=F32, precision=lax.Precision.HIGHEST)


def _fdot_nt(a, b):
    return lax.dot_general(a, b, (((1,), (1,)), ((), ())), preferred_element_type=F32,
                           precision=lax.Precision.HIGHEST)


def _layer_spec(shape, layer, *, single=False):
    nd = len(shape)
    kw = dict(pipeline_mode=pl.Buffered(1)) if single else {}
    return pl.BlockSpec((None,) + tuple(shape[1:]), lambda *_: (layer,) + (0,) * (nd - 1), **kw)


class _Tiling:
    def __init__(self, bn, l, ns, t):
        assert bn % ns == 0 and l % t == 0
        self.ns, self.t, self.r = ns, t, ns * t
        self.nt = l // t
        self.n = (bn // ns) * self.nt

    def tile_a(self, g):
        return jnp.minimum(g, self.n - 1)

    def tile_b(self, g):
        return jnp.maximum(g - 1, 0)

    def rows(self, width, stage):
        tile = self.tile_a if stage == "a" else self.tile_b
        return pl.BlockSpec((self.ns, self.t, width),
                            lambda g: (tile(g) // self.nt, tile(g) % self.nt, 0))

    def state(self, shape, layer, stage):
        tile = self.tile_a if stage == "a" else self.tile_b
        nd = len(shape)
        return pl.BlockSpec((None, self.ns) + tuple(shape[2:]),
                            lambda g: (layer, tile(g) // self.nt) + (0,) * (nd - 2))


def _cparams():
    return pltpu.CompilerParams(dimension_semantics=("arbitrary",), vmem_limit_bytes=VMEM_LIMIT)


def _interleave(*stages):
    live = list(stages)
    while live:
        for st in list(live):
            try:
                next(st)
            except StopIteration:
                live.remove(st)


def _ffn_steps(x_fn, gpre_ref, wup_ref, wdn_ref, gpost_ref, hid_ref, finish):
    x = x_fn()
    h = _rms(x, gpre_ref[...]).astype(BF16)
    yield
    for j in range(D_FF // FF_TILE):
        g = jnp.dot(h, wup_ref[:, j * FF_TILE:(j + 1) * FF_TILE], preferred_element_type=F32)
        u = jnp.dot(h, wup_ref[:, D_FF + j * FF_TILE:D_FF + (j + 1) * FF_TILE], preferred_element_type=F32)
        hid_ref[:, j * FF_TILE:(j + 1) * FF_TILE] = (_silu(g) * u).astype(BF16)
        yield
    ys = []
    for c in range(D_MODEL // FF_TILE):
        ys.append(jnp.dot(hid_ref[...], wdn_ref[:, c * FF_TILE:(c + 1) * FF_TILE], preferred_element_type=F32))
        yield
    finish(x + 0.5 * _rms(jnp.concatenate(ys, axis=1), gpost_ref[...]))


def _prep_steps(x_fn, w, outs, ncst_ref, cscr_ref, ns, t):
    gpre_ref, wmain_ref, wscal_ref, cw_ref, cb_ref, wrg_ref, brg_ref, lam_ref, alog_ref, dtb_ref = w
    a_ref, b_ref, rgate_ref, q_ref, k_ref, v_ref, zg_ref, scal_ref = outs
    r = ns * t
    cw_n = FF_TILE
    h = _rms(x_fn(), gpre_ref[...]).astype(BF16)
    yield

    def proj_cols(c0):
        return jnp.dot(h, wmain_ref[:, c0:c0 + cw_n], preferred_element_type=F32)

    def conv_cols(c0):
        cols = slice(c0, c0 + cw_n)
        cscr_ref[:, CONV_PAD:CONV_PAD + t, cols] = proj_cols(c0).reshape(ns, t, cw_n)
        conv = None
        for j in range(CONV_WIDTH):
            lo = CONV_PAD - (CONV_WIDTH - 1) + j
            term = cscr_ref[:, lo:lo + t, cols] * cw_ref[j:j + 1, cols]
            conv = term if conv is None else conv + term
        hist = cscr_ref[:, CONV_PAD + t - (CONV_WIDTH - 1):CONV_PAD + t, cols]
        ncst_ref[:, :, cols] = hist
        cscr_ref[:, CONV_PAD - (CONV_WIDTH - 1):CONV_PAD, cols] = hist
        return conv.reshape(r, cw_n)

    xr_parts = []
    for c0 in range(0, RG_WIDTH, cw_n):
        xr_parts.append(conv_cols(c0) + cb_ref[:, c0:c0 + cw_n])
        yield
    xr = jnp.concatenate(xr_parts, axis=1)
    gax = jnp.dot(xr.astype(BF16), wrg_ref[...], preferred_element_type=F32) + brg_ref[...]
    rr = jax.nn.sigmoid(gax[:, :RG_WIDTH])
    ii = jax.nn.sigmoid(gax[:, RG_WIDTH:])
    log_a = -RG_C * rr * _softplus(-lam_ref[...])
    a = jnp.exp(log_a)
    a_ref[...] = a.reshape(ns, t, RG_WIDTH)
    one_minus_a2 = -jnp.tanh(log_a) * (a * a + 1.0)
    b_ref[...] = (jnp.sqrt(one_minus_a2) * (ii * xr)).reshape(ns, t, RG_WIDTH)
    yield

    for base, ref, scale in ((RG_WIDTH, q_ref, DN_HEAD_DIM ** -0.5), (RG_WIDTH + DN_WIDTH, k_ref, None)):
        for c0 in range(0, DN_WIDTH, cw_n):
            act = _silu(conv_cols(base + c0))
            for hh in range(cw_n // DN_HEAD_DIM):
                xh = act[:, hh * DN_HEAD_DIM:(hh + 1) * DN_HEAD_DIM]
                xn = xh * lax.rsqrt(jnp.sum(xh * xh, axis=-1, keepdims=True) + EPS)
                if scale is not None:
                    xn = xn * scale
                lo = c0 + hh * DN_HEAD_DIM
                ref[:, :, lo:lo + DN_HEAD_DIM] = xn.reshape(ns, t, DN_HEAD_DIM)
            yield
    for c0 in range(0, DN_WIDTH, cw_n):
        v_ref[:, :, c0:c0 + cw_n] = _silu(conv_cols(RG_WIDTH + 2 * DN_WIDTH + c0)).reshape(ns, t, cw_n)
        yield

    for c0 in range(0, RG_WIDTH, cw_n):
        rgate_ref[:, :, c0:c0 + cw_n] = jax.nn.gelu(proj_cols(CONV_CH + c0)).reshape(ns, t, cw_n)
        yield
    for c0 in range(0, DN_WIDTH, cw_n):
        zg_ref[:, :, c0:c0 + cw_n] = _silu(proj_cols(CONV_CH + RG_WIDTH + c0)).reshape(ns, t, cw_n)
        yield

    scal = jnp.dot(h, wscal_ref[...], preferred_element_type=F32)
    lane = lax.broadcasted_iota(jnp.int32, scal.shape, 1)
    beta = jax.nn.sigmoid(scal)
    g = -jnp.exp(alog_ref[...]) * _softplus(scal + dtb_ref[...])
    scal_ref[...] = jnp.where(lane < DN_HEADS, beta, g).reshape(ns, t, LANES)


def _recur_steps(ins, nw_ref, hout_ref, sout_ref, mix_ref, ns, t):
    a_ref, b_ref, rgate_ref, q_ref, k_ref, v_ref, zg_ref, scal_ref = ins
    r = ns * t
    seg = min(CHUNK, t)
    nseg = CHUNK // seg
    nchunk = r // CHUNK

    rowmod = lax.broadcasted_iota(jnp.int32, (t // SUBLANES, SUBLANES, RG_WIDTH), 1)

    def rg_seq(si, carry):
        a = a_ref[si].reshape(t // SUBLANES, SUBLANES, RG_WIDTH)
        b = b_ref[si].reshape(t // SUBLANES, SUBLANES, RG_WIDTH)
        for s in (1, 2, 4):
            m = rowmod >= s
            a_sh = pltpu.roll(a, s, axis=1)
            b_sh = pltpu.roll(b, s, axis=1)
            b = jnp.where(m, a * b_sh + b, b)
            a = jnp.where(m, a * a_sh, a)
        hc = hout_ref[si]
        for gi in range(t // SUBLANES):
            rows = slice(gi * SUBLANES, (gi + 1) * SUBLANES)
            hb = a[gi] * hc + b[gi]
            mix_ref[si, rows, :RG_WIDTH] = (hb * rgate_ref[si, rows, :]).astype(mix_ref.dtype)
            hc = hb[SUBLANES - 1:SUBLANES, :]
        hout_ref[si] = hc
        return carry

    lax.fori_loop(0, ns, rg_seq, 0)
    yield

    ri = lax.broadcasted_iota(jnp.int32, (CHUNK, CHUNK), 0)
    ci = lax.broadcasted_iota(jnp.int32, (CHUNK, CHUNK), 1)
    seg_shift = seg.bit_length() - 1
    same = (ri >> seg_shift) == (ci >> seg_shift)
    incl = jnp.logical_and(same, ri >= ci)
    strict = jnp.logical_and(same, ri > ci)
    incl_f = incl.astype(F32)
    eye = (ri == ci).astype(F32)
    sel = (lax.broadcasted_iota(jnp.int32, (SUBLANES, LANES), 0)
           == lax.broadcasted_iota(jnp.int32, (SUBLANES, LANES), 1)).astype(F32)
    nw = nw_ref[...]

    def chunk_index(c):
        if t >= CHUNK:
            per_seq = t // CHUNK
            return c // per_seq, slice((c % per_seq) * CHUNK, (c % per_seq + 1) * CHUNK)
        per_chunk = CHUNK // t
        return slice(c * per_chunk, (c + 1) * per_chunk), slice(None)

    def load_rows(ref, c, lanes):
        i0, i1 = chunk_index(c)
        return ref[i0, i1, lanes].reshape(CHUNK, -1)

    def store_rows(ref, c, lanes, val):
        i0, i1 = chunk_index(c)
        ref[i0, i1, lanes] = val if t >= CHUNK else val.reshape(CHUNK // t, t, -1)

    def state_index(c, jj):
        return c // (t // CHUNK) if t >= CHUNK else c * nseg + jj

    chains = []
    for c in range(nchunk):
        sc = load_rows(scal_ref, c, slice(None))
        gall = _fdot(incl_f, sc)
        gt = _fdot_nt(sel, gall)
        for hh in range(DN_HEADS):
            sl = slice(hh * DN_HEAD_DIM, (hh + 1) * DN_HEAD_DIM)
            chains.append(dict(
                c=c, hh=hh, sl=sl, beta=sc[:, hh:hh + 1],
                gc=gall[:, DN_HEADS + hh:DN_HEADS + hh + 1],
                gr=gt[DN_HEADS + hh:DN_HEADS + hh + 1, :]))
    yield
    for ch in chains:
        qh = load_rows(q_ref, ch["c"], ch["sl"])
        kh = load_rows(k_ref, ch["c"], ch["sl"])
        dec = jnp.where(incl, jnp.exp(jnp.where(incl, ch["gc"] - ch["gr"], 0.0)), 0.0)
        kb = kh * ch["beta"]
        kq = _bdot_nt(jnp.concatenate([kb, qh], axis=0), kh)
        ch.update(qh=qh, kh=kh, kb=kb,
                  x=jnp.where(strict, kq[:CHUNK] * dec, 0.0),
                  att=kq[CHUNK:] * dec)
    for ch in chains:
        ch["p"] = eye - ch["x"]
        if seg > 2:
            ch["x"] = _bdot(ch["x"], ch["x"])
    yield
    span = 4
    while span < seg:
        for ch in chains:
            both = _bdot(jnp.concatenate([ch["x"], ch["p"]], axis=0), ch["x"])
            ch["x"], ch["p"] = both[:CHUNK], ch["p"] + both[CHUNK:]
        yield
        span *= 2
    if seg > 2:
        for ch in chains:
            ch["p"] = ch["p"] + _bdot(ch["p"], ch["x"])
        yield
    for ch in chains:
        vh = load_rows(v_ref, ch["c"], ch["sl"])
        eg = jnp.exp(ch["gc"])
        sol = _bdot(ch["p"], jnp.concatenate([vh * ch["beta"], ch["kb"] * eg], axis=1))
        ch.update(u=sol[:, :DN_HEAD_DIM], w=sol[:, DN_HEAD_DIM:], qe=ch["qh"] * eg)
    yield
    for ch in chains:
        ch["s_old"], ch["rr"] = [], []
        for jj in range(nseg):
            rs = slice(jj * seg, (jj + 1) * seg)
            s_old = sout_ref[state_index(ch["c"], jj), ch["hh"]]
            ch["s_old"].append(s_old)
            ch["rr"].append(_bdot(jnp.concatenate([ch["w"][rs], ch["qe"][rs]], axis=0), s_old))
    yield
    for ch in chains:
        vnews = []
        for jj in range(nseg):
            rs = slice(jj * seg, (jj + 1) * seg)
            vnew = ch["u"][rs] - ch["rr"][jj][:seg]
            gl = ch["gc"][(jj + 1) * seg - 1:(jj + 1) * seg, :]
            kdec = ch["kh"][rs] * jnp.exp(gl - ch["gc"][rs])
            if nseg == 1:
                both = _bdot(jnp.concatenate([ch["att"], kdec.T], axis=0), vnew)
                ch["ov"], upd = both[:CHUNK], both[CHUNK:]
            else:
                upd = _bdot(kdec.T, vnew)
            sout_ref[state_index(ch["c"], jj), ch["hh"]] = ch["s_old"][jj] * jnp.exp(gl) + upd
            vnews.append(vnew)
        if nseg > 1:
            ch["ov"] = _bdot(ch["att"], jnp.concatenate(vnews, axis=0))
    yield
    for ch in chains:
        oint = ch["rr"][0][seg:] if nseg == 1 else jnp.concatenate([x[seg:] for x in ch["rr"]], axis=0)
        o = oint + ch["ov"]
        o = _rms(o, nw) * load_rows(zg_ref, ch["c"], ch["sl"])
        store_rows(mix_ref, ch["c"], slice(RG_WIDTH + ch["hh"] * DN_HEAD_DIM,
                                           RG_WIDTH + (ch["hh"] + 1) * DN_HEAD_DIM), o.astype(mix_ref.dtype))


def _kernel_a(x_ref, gpre1_ref, wup_ref, wdn_ref, gpost1_ref,
              gprem_ref, wmain_ref, wscal_ref, cw_ref, cb_ref, wrg_ref, brg_ref, lam_ref, alog_ref, dtb_ref,
              cst_ref, _prev_conv, *rest, tl):
    x1_ref, prep_ref, ncst_ref, hid_ref, xs_ref, cscr_ref = rest
    outs = _prep_views(prep_ref)
    g = pl.program_id(0)
    hist_rows = slice(CONV_PAD - (CONV_WIDTH - 1), CONV_PAD)

    @pl.when(g == 0)
    def _():
        xs_ref[...] = jnp.zeros_like(xs_ref)
        cscr_ref[...] = jnp.zeros_like(cscr_ref)

    @pl.when((g - 1) % tl.nt == 0)
    def _():
        cscr_ref[:, hist_rows, :] = cst_ref[...]

    def finish(x1):
        x1_ref[...] = x1.reshape(tl.ns, tl.t, D_MODEL)
        xs_ref[g % 2] = x1

    w = (gprem_ref, wmain_ref, wscal_ref, cw_ref, cb_ref, wrg_ref, brg_ref, lam_ref, alog_ref, dtb_ref)
    _interleave(
        _prep_steps(lambda: xs_ref[(g + 1) % 2], w, outs, ncst_ref, cscr_ref, tl.ns, tl.t),
        _ffn_steps(lambda: x_ref[...].reshape(tl.r, D_MODEL), gpre1_ref, wup_ref, wdn_ref, gpost1_ref, hid_ref,
                   finish))


def _call_a(x, layer, conv_state, prev_conv, w, *, ns, t):
    bn, l, _ = x.shape
    tl = _Tiling(bn, l, ns, t)
    f1, pw = w["f1"], w["prep"]
    big = {"wup", "wdn", "wmain", "wrg"}
    names = ["gpre", "wmain", "wscal", "cw", "cb", "wrg", "brg", "lam", "alog", "dtb"]
    args = [x, *f1] + [pw[n] for n in names] + [conv_state]
    in_specs = ([tl.rows(D_MODEL, "a"), _layer_spec(f1[0].shape, layer), _layer_spec(f1[1].shape, layer, single=True),
                 _layer_spec(f1[2].shape, layer, single=True), _layer_spec(f1[3].shape, layer)]
                + [_layer_spec(pw[n].shape, layer, single=n in big) for n in names]
                + [tl.state(conv_state.shape, layer, "b")])
    in_specs[5 + names.index("wmain")] = pl.BlockSpec(
        (None, D_MODEL, MAIN_COLS), lambda g: (layer, 0, 0), pipeline_mode=pl.Buffered(1))
    aliases = {len(args): 2}
    args.append(prev_conv)
    in_specs.append(pl.BlockSpec(memory_space=pl.ANY))
    return pl.pallas_call(
        functools.partial(_kernel_a, tl=tl),
        grid=(tl.n + 1,),
        in_specs=in_specs,
        out_specs=[tl.rows(D_MODEL, "a"), tl.rows(PREP_COLS, "b"), tl.state(conv_state.shape, layer, "b")],
        out_shape=[jax.ShapeDtypeStruct((bn, l, D_MODEL), F32), jax.ShapeDtypeStruct((bn, l, PREP_COLS), F32),
                   jax.ShapeDtypeStruct(conv_state.shape, F32)],
        scratch_shapes=[pltpu.VMEM((tl.r, D_FF), BF16), pltpu.VMEM((2, tl.r, D_MODEL), F32),
                        pltpu.VMEM((ns, t + CONV_PAD, CONV_CH), F32)],
        input_output_aliases=aliases,
        compiler_params=_cparams(),
        name="ffn1_prep",
    )(*args)


def _kernel_b(prep_ref, h0_ref, s0_ref, nw_ref, x1_ref, wout_ref, gmix_ref, _prev_h, _prev_s,
              xmid_ref, hout_ref, sout_ref, mix_ref, *, tl):
    g = pl.program_id(0)

    @pl.when(g % tl.nt == 0)
    def _():
        hout_ref[...] = h0_ref[...]
        sout_ref[...] = s0_ref[...]

    _interleave(_recur_steps(_prep_views(prep_ref), nw_ref, hout_ref, sout_ref, mix_ref, tl.ns, tl.t))
    m = mix_ref[...].reshape(tl.r, D_MODEL).astype(BF16)
    y = jnp.dot(m, wout_ref[...], preferred_element_type=F32)
    xm = x1_ref[...].reshape(tl.r, D_MODEL) + _rms(y, gmix_ref[...])
    xmid_ref[...] = xm.reshape(tl.ns, tl.t, D_MODEL)


def _call_b(prep, x1, layer, h0, s0, prev_h, prev_s, w, *, ns, t):
    bn, l, _ = prep.shape
    tl = _Tiling(bn, l, ns, t)
    args = [prep, h0, s0, w["nw"], x1, w["wout"], w["gmix"]]
    in_specs = [tl.rows(PREP_COLS, "a"), tl.state(h0.shape, layer, "a"), tl.state(s0.shape, layer, "a"),
                _layer_spec(w["nw"].shape, layer), tl.rows(D_MODEL, "a"),
                _layer_spec(w["wout"].shape, layer, single=True), _layer_spec(w["gmix"].shape, layer)]
    aliases = {len(args): 1, len(args) + 1: 2}
    args += [prev_h, prev_s]
    in_specs += [pl.BlockSpec(memory_space=pl.ANY)] * 2
    return pl.pallas_call(
        functools.partial(_kernel_b, tl=tl),
        grid=(tl.n,),
        in_specs=in_specs,
        out_specs=[tl.rows(D_MODEL, "a"), tl.state(h0.shape, layer, "a"), tl.state(s0.shape, layer, "a")],
        out_shape=[jax.ShapeDtypeStruct((bn, l, D_MODEL), F32), jax.ShapeDtypeStruct(h0.shape, F32),
                   jax.ShapeDtypeStruct(s0.shape, F32)],
        scratch_shapes=[pltpu.VMEM((ns, t, D_MODEL), F32)],
        input_output_aliases=aliases,
        compiler_params=_cparams(),
        name="recur_oproj",
    )(*args)


def _kernel_c(xp_ref, xs_ref, gpre_ref, wup_ref, wdn_ref, gpost_ref, *rest, nbp, final):
    rest = list(rest)
    gfin_ref = rest.pop(0) if final else None
    yp_ref, ys_ref, hid_ref = rest
    is_prompt = pl.program_id(0) < nbp

    def finish(out):
        if final:
            out = _rms(out, gfin_ref[...])

        @pl.when(is_prompt)
        def _():
            yp_ref[...] = out

        @pl.when(jnp.logical_not(is_prompt))
        def _():
            ys_ref[...] = out

    _interleave(_ffn_steps(lambda: jnp.where(is_prompt, xp_ref[...], xs_ref[...]),
                           gpre_ref, wup_ref, wdn_ref, gpost_ref, hid_ref, finish))


def _call_c(x_p, x_s, layer, w, gfin):
    n_p, n_s = x_p.shape[0], x_s.shape[0]
    nbp, nbs = n_p // TM, n_s // TM
    f2 = w["f2"]
    p_spec = pl.BlockSpec((TM, D_MODEL), lambda i: (jnp.minimum(i, nbp - 1), 0))
    s_spec = pl.BlockSpec((TM, D_MODEL), lambda i: (jnp.maximum(i - nbp, 0), 0))
    args = [x_p, x_s, *f2]
    in_specs = [p_spec, s_spec, _layer_spec(f2[0].shape, layer), _layer_spec(f2[1].shape, layer, single=True),
                _layer_spec(f2[2].shape, layer, single=True), _layer_spec(f2[3].shape, layer)]
    if gfin is not None:
        args.append(gfin)
        in_specs.append(pl.BlockSpec(gfin.shape, lambda i: (0, 0)))
    return pl.pallas_call(
        functools.partial(_kernel_c, nbp=nbp, final=gfin is not None),
        grid=(nbp + nbs,),
        in_specs=in_specs,
        out_specs=[p_spec, s_spec],
        out_shape=[jax.ShapeDtypeStruct((n_p, D_MODEL), F32), jax.ShapeDtypeStruct((n_s, D_MODEL), F32)],
        scratch_shapes=[pltpu.VMEM((TM, D_FF), BF16)],
        compiler_params=_cparams(),
        name="ffn2",
    )(*args)


def _lane_pad(mat, offset):
    depth, n = mat.shape
    out = jnp.zeros((depth, 1, LANES), F32)
    return lax.dynamic_update_slice(out, mat.reshape(depth, 1, n).astype(F32), (0, 0, offset))


def _weights(ffn1_norm_pre, ffn1_w_up, ffn1_w_down, ffn1_norm_post, mix_norm_pre, w_in, conv_w, conv_b_rg,
             rg_w_a, rg_b_a, rg_w_x, rg_b_x, rg_lambda, dn_a_log, dn_dt_bias, dn_norm_w, w_out, mix_norm_post,
             ffn2_norm_pre, ffn2_w_up, ffn2_w_down, ffn2_norm_post):
    depth = w_in.shape[0]
    row = lambda v: v.reshape(depth, 1, -1)
    wscal = jnp.zeros((depth, D_MODEL, LANES), BF16).at[:, :, :2 * DN_HEADS].set(
        w_in[:, :, MAIN_COLS:].astype(BF16))
    eye = jnp.eye(RG_BLOCKS, dtype=F32)
    bd = lambda w: jnp.einsum('lhij,hg->lhigj', w, eye).reshape(depth, RG_WIDTH, RG_WIDTH)
    return dict(
        f1=(row(ffn1_norm_pre), ffn1_w_up.astype(BF16), ffn1_w_down.astype(BF16), row(ffn1_norm_post)),
        f2=(row(ffn2_norm_pre), ffn2_w_up.astype(BF16), ffn2_w_down.astype(BF16), row(ffn2_norm_post)),
        prep=dict(gpre=row(mix_norm_pre), wmain=w_in.astype(BF16), wscal=wscal, cw=conv_w,
                  cb=row(conv_b_rg), wrg=jnp.concatenate([bd(rg_w_a), bd(rg_w_x)], axis=2).astype(BF16),
                  brg=jnp.concatenate([row(rg_b_a), row(rg_b_x)], axis=2), lam=row(rg_lambda),
                  alog=_lane_pad(dn_a_log, DN_HEADS), dtb=_lane_pad(dn_dt_bias, DN_HEADS)),
        nw=row(dn_norm_w),
        wout=w_out.astype(BF16),
        gmix=row(mix_norm_post),
    )


def kernel(x_prompt, x_sample, state_conv, state_rglru, state_delta, ffn1_norm_pre, ffn1_w_up, ffn1_w_down, ffn1_norm_post, mix_norm_pre, w_in, conv_w, conv_b_rg, rg_w_a, rg_b_a, rg_w_x, rg_b_x, rg_lambda, dn_a_log, dn_dt_bias, dn_norm_w, w_out, mix_norm_post, ffn2_norm_pre, ffn2_w_up, ffn2_w_down, ffn2_norm_post, final_norm):
    depth = w_in.shape[0]
    w = _weights(ffn1_norm_pre, ffn1_w_up, ffn1_w_down, ffn1_norm_post, mix_norm_pre, w_in, conv_w, conv_b_rg,
                 rg_w_a, rg_b_a, rg_w_x, rg_b_x, rg_lambda, dn_a_log, dn_dt_bias, dn_norm_w, w_out,
                 mix_norm_post, ffn2_norm_pre, ffn2_w_up, ffn2_w_down, ffn2_norm_post)
    bp, lp, _ = x_prompt.shape
    bs, ls, _ = x_sample.shape
    dt = x_prompt.dtype
    pops = [
        dict(x=x_prompt, blk_a=(4, CHUNK), blk_b=(bp, CHUNK),
             conv=jnp.zeros((depth, bp, CONV_WIDTH - 1, CONV_CH), dt),
             rg=jnp.zeros((depth, bp, 1, RG_WIDTH), dt),
             dn=jnp.zeros((depth, bp, DN_HEADS, DN_HEAD_DIM, DN_HEAD_DIM), dt)),
        dict(x=x_sample, blk_a=(4 * CHUNK // ls, ls), blk_b=(2 * CHUNK // ls, ls),
             conv=state_conv, rg=state_rglru.reshape(depth, bs, 1, RG_WIDTH), dn=state_delta),
    ]
    for p in pops:
        p.update(new_conv=jnp.zeros_like(p["conv"]), new_rg=jnp.zeros_like(p["rg"]),
                 new_dn=jnp.zeros_like(p["dn"]))
    pp, ps = pops
    flat = lambda v: v.reshape(-1, D_MODEL)
    for li in range(depth):
        for p in pops:
            p["x1"], prep, p["new_conv"] = _call_a(p["x"], li, p["conv"], p["new_conv"], w,
                                                   ns=p["blk_a"][0], t=p["blk_a"][1])
            p["xm"], p["new_rg"], p["new_dn"] = _call_b(prep, p["x1"], li, p["rg"], p["dn"], p["new_rg"],
                                                        p["new_dn"], w, ns=p["blk_b"][0], t=p["blk_b"][1])
        gfin = final_norm.reshape(1, -1) if li == depth - 1 else None
        y_p, y_s = _call_c(flat(pp["xm"]), flat(ps["xm"]), li, w, gfin)
        pp["x"], ps["x"] = y_p.reshape(pp["x"].shape), y_s.reshape(ps["x"].shape)
    return (pp["x"], ps["x"], pp["new_conv"], pp["new_rg"].reshape(depth, bp, RG_WIDTH), pp["new_dn"],
            ps["new_conv"], ps["new_rg"].reshape(depth, bs, RG_WIDTH), ps["new_dn"])
```

```python
import functools

import jax
import jax.numpy as jnp
from jax import lax
from jax.experimental import pallas as pl
from jax.experimental.pallas import tpu as pltpu

F32 = jnp.float32
BF16 = jnp.bfloat16

D_MODEL = 1024
D_FF = 2816
RG_WIDTH = 512
RG_BLOCKS = 8
RG_BLOCK = RG_WIDTH // RG_BLOCKS
RG_C = 8.0
DN_HEADS = 4
DN_HEAD_DIM = 128
DN_WIDTH = DN_HEADS * DN_HEAD_DIM
CONV_WIDTH = 4
CONV_CH = RG_WIDTH + 3 * DN_WIDTH
GATE_COLS = RG_WIDTH + DN_WIDTH
MAIN_COLS = CONV_CH + GATE_COLS
EPS = 1e-6

LANES = 128
SUBLANES = 8
CHUNK = 64
CONV_PAD = SUBLANES
FF_TILE = 256
TM = 512
VMEM_LIMIT = 56 * 1024 * 1024
PREP_WIDTHS = (RG_WIDTH, RG_WIDTH, RG_WIDTH, DN_WIDTH, DN_WIDTH, DN_WIDTH, DN_WIDTH, LANES)
PREP_COLS = sum(PREP_WIDTHS)


def _prep_views(prep_ref):
    views, off = [], 0
    for width in PREP_WIDTHS:
        views.append(prep_ref.at[:, :, off:off + width])
        off += width
    return tuple(views)


def _rms(x, g):
    return x * lax.rsqrt(jnp.mean(x * x, axis=-1, keepdims=True) + EPS) * g


def _sigmoid(x):
    return 0.5 + 0.5 * jnp.tanh(0.5 * x)


def _silu(x):
    hx = 0.5 * x
    return hx + hx * jnp.tanh(hx)


def _softplus(x):
    return jnp.maximum(x, 0.0) + jnp.log1p(jnp.exp(-jnp.abs(x)))


def _bdot(a, b):
    return jnp.dot(a.astype(BF16), b.astype(BF16), preferred_element_type=F32)


def _bdot_nt(a, b):
    return lax.dot_general(a.astype(BF16), b.astype(BF16), (((1,), (1,)), ((), ())),
                           preferred_element_type=F32)


def _fdot(a, b):
    return jnp.dot(a, b, preferred_element_type=F32, precision=lax.Precision.HIGHEST)


def _fdot_nt(a, b):
    return lax.dot_general(a, b, (((1,), (1,)), ((), ())), preferred_element_type=F32,
                           precision=lax.Precision.HIGHEST)


def _layer_spec(shape, layer, *, single=False):
    nd = len(shape)
    kw = dict(pipeline_mode=pl.Buffered(1)) if single else {}
    return pl.BlockSpec((None,) + tuple(shape[1:]), lambda *_: (layer,) + (0,) * (nd - 1), **kw)


class _Tiling:
    def __init__(self, bn, l, ns, t):
        assert bn % ns == 0 and l % t == 0
        self.ns, self.t, self.r = ns, t, ns * t
        self.nt = l // t
        self.n = (bn // ns) * self.nt

    def tile_a(self, g):
        return jnp.minimum(g, self.n - 1)

    def tile_b(self, g):
        return jnp.maximum(g - 1, 0)

    def rows(self, width, stage):
        tile = self.tile_a if stage == "a" else self.tile_b
        return pl.BlockSpec((self.ns, self.t, width),
                            lambda g: (tile(g) // self.nt, tile(g) % self.nt, 0))

    def state(self, shape, layer, stage):
        tile = self.tile_a if stage == "a" else self.tile_b
        nd = len(shape)
        return pl.BlockSpec((None, self.ns) + tuple(shape[2:]),
                            lambda g: (layer, tile(g) // self.nt) + (0,) * (nd - 2))


def _cparams():
    return pltpu.CompilerParams(dimension_semantics=("arbitrary",), vmem_limit_bytes=VMEM_LIMIT)


def _interleave(*stages):
    live = list(stages)
    while live:
        for st in list(live):
            try:
                next(st)
            except StopIteration:
                live.remove(st)


def _ffn_steps(x_fn, gpre_ref, wup_ref, wdn_ref, gpost_ref, hid_ref, finish):
    x = x_fn()
    h = _rms(x, gpre_ref[...]).astype(BF16)
    yield
    for j in range(D_FF // FF_TILE):
        g = jnp.dot(h, wup_ref[:, j * FF_TILE:(j + 1) * FF_TILE], preferred_element_type=F32)
        u = jnp.dot(h, wup_ref[:, D_FF + j * FF_TILE:D_FF + (j + 1) * FF_TILE], preferred_element_type=F32)
        hid_ref[:, j * FF_TILE:(j + 1) * FF_TILE] = (_silu(g) * u).astype(BF16)
        yield
    ys = []
    for c in range(D_MODEL // FF_TILE):
        ys.append(jnp.dot(hid_ref[...], wdn_ref[:, c * FF_TILE:(c + 1) * FF_TILE], preferred_element_type=F32))
        yield
    finish(x + 0.5 * _rms(jnp.concatenate(ys, axis=1), gpost_ref[...]))


def _prep_steps(x_fn, w, outs, ncst_ref, cscr_ref, ns, t):
    gpre_ref, wmain_ref, wscal_ref, cw_ref, cb_ref, wrg_ref, brg_ref, lam_ref, alog_ref, dtb_ref = w
    a_ref, b_ref, rgate_ref, q_ref, k_ref, v_ref, zg_ref, scal_ref = outs
    r = ns * t
    cw_n = FF_TILE
    h = _rms(x_fn(), gpre_ref[...]).astype(BF16)
    yield

    def proj_cols(c0):
        return jnp.dot(h, wmain_ref[:, c0:c0 + cw_n], preferred_element_type=F32)

    def conv_cols(c0):
        cols = slice(c0, c0 + cw_n)
        x = proj_cols(c0).reshape(ns, t, cw_n)
        hist = cscr_ref[:, :, cols]
        taps = [cw_ref[j:j + 1, cols] for j in range(CONV_WIDTH)]
        u, hu = taps[0] * x, taps[0] * hist
        for j in range(1, CONV_WIDTH):
            sh = pltpu.roll(jnp.concatenate([hu, u], axis=1), 1, axis=1)
            u, hu = taps[j] * x + sh[:, CONV_PAD:], taps[j] * hist + sh[:, :CONV_PAD]
        new_hist = x[:, t - (CONV_WIDTH - 1):, :]
        ncst_ref[:, :, cols] = new_hist
        cscr_ref[:, CONV_PAD - (CONV_WIDTH - 1):, cols] = new_hist
        return u.reshape(r, cw_n)

    xr_parts = []
    for c0 in range(0, RG_WIDTH, cw_n):
        xr_parts.append(conv_cols(c0) + cb_ref[:, c0:c0 + cw_n])
        yield
    xr = jnp.concatenate(xr_parts, axis=1)
    gax = jnp.dot(xr.astype(BF16), wrg_ref[...], preferred_element_type=F32) + brg_ref[...]
    rr = _sigmoid(gax[:, :RG_WIDTH])
    ii = _sigmoid(gax[:, RG_WIDTH:])
    log_a = -RG_C * rr * _softplus(-lam_ref[...])
    a = jnp.exp(log_a)
    a_ref[...] = a.reshape(ns, t, RG_WIDTH)
    one_minus_a2 = -jnp.tanh(log_a) * (a * a + 1.0)
    b_ref[...] = (jnp.sqrt(one_minus_a2) * (ii * xr)).reshape(ns, t, RG_WIDTH)
    yield

    for base, ref, scale in ((RG_WIDTH, q_ref, DN_HEAD_DIM ** -0.5), (RG_WIDTH + DN_WIDTH, k_ref, None)):
        for c0 in range(0, DN_WIDTH, cw_n):
            act = _silu(conv_cols(base + c0))
            for hh in range(cw_n // DN_HEAD_DIM):
                xh = act[:, hh * DN_HEAD_DIM:(hh + 1) * DN_HEAD_DIM]
                inv = lax.rsqrt(jnp.sum(xh * xh, axis=-1, keepdims=True) + EPS)
                xn = xh * (inv if scale is None else inv * scale)
                lo = c0 + hh * DN_HEAD_DIM
                ref[:, :, lo:lo + DN_HEAD_DIM] = xn.reshape(ns, t, DN_HEAD_DIM)
            yield
    for c0 in range(0, DN_WIDTH, cw_n):
        v_ref[:, :, c0:c0 + cw_n] = _silu(conv_cols(RG_WIDTH + 2 * DN_WIDTH + c0)).reshape(ns, t, cw_n)
        yield

    for c0 in range(0, RG_WIDTH, cw_n):
        rgate_ref[:, :, c0:c0 + cw_n] = jax.nn.gelu(proj_cols(CONV_CH + c0)).reshape(ns, t, cw_n)
        yield
    for c0 in range(0, DN_WIDTH, cw_n):
        zg_ref[:, :, c0:c0 + cw_n] = _silu(proj_cols(CONV_CH + RG_WIDTH + c0)).reshape(ns, t, cw_n)
        yield

    scal = jnp.dot(h, wscal_ref[...], preferred_element_type=F32)
    lane = lax.broadcasted_iota(jnp.int32, scal.shape, 1)
    beta = _sigmoid(scal)
    g = -jnp.exp(alog_ref[...]) * _softplus(scal + dtb_ref[...])
    scal_ref[...] = jnp.where(lane < DN_HEADS, beta, g).reshape(ns, t, LANES)


def _recur_steps(ins, nw_ref, hout_ref, sout_ref, mix_ref, ns, t):
    a_ref, b_ref, rgate_ref, q_ref, k_ref, v_ref, zg_ref, scal_ref = ins
    r = ns * t
    seg = min(CHUNK, t)
    nseg = CHUNK // seg
    nchunk = r // CHUNK

    rowmod = lax.broadcasted_iota(jnp.int32, (t // SUBLANES, SUBLANES, RG_WIDTH), 1)

    def rg_seq(si, carry):
        a = a_ref[si].reshape(t // SUBLANES, SUBLANES, RG_WIDTH)
        b = b_ref[si].reshape(t // SUBLANES, SUBLANES, RG_WIDTH)
        for s in (1, 2, 4):
            m = rowmod >= s
            a_sh = pltpu.roll(a, s, axis=1)
            b_sh = pltpu.roll(b, s, axis=1)
            b = jnp.where(m, a * b_sh + b, b)
            a = jnp.where(m, a * a_sh, a)
        hc = hout_ref[si]
        for gi in range(t // SUBLANES):
            rows = slice(gi * SUBLANES, (gi + 1) * SUBLANES)
            hb = a[gi] * hc + b[gi]
            mix_ref[si, rows, :RG_WIDTH] = (hb * rgate_ref[si, rows, :]).astype(mix_ref.dtype)
            hc = hb[SUBLANES - 1:SUBLANES, :]
        hout_ref[si] = hc
        return carry

    lax.fori_loop(0, ns, rg_seq, 0)
    yield

    ri = lax.broadcasted_iota(jnp.int32, (CHUNK, CHUNK), 0)
    ci = lax.broadcasted_iota(jnp.int32, (CHUNK, CHUNK), 1)
    seg_shift = seg.bit_length() - 1
    same = (ri >> seg_shift) == (ci >> seg_shift)
    incl = jnp.logical_and(same, ri >= ci)
    strict = jnp.logical_and(same, ri > ci)
    incl_f = incl.astype(F32)
    eye = (ri == ci).astype(F32)
    sel = (lax.broadcasted_iota(jnp.int32, (SUBLANES, LANES), 0)
           == lax.broadcasted_iota(jnp.int32, (SUBLANES, LANES), 1)).astype(F32)
    nw = nw_ref[...]

    def chunk_index(c):
        if t >= CHUNK:
            per_seq = t // CHUNK
            return c // per_seq, slice((c % per_seq) * CHUNK, (c % per_seq + 1) * CHUNK)
        per_chunk = CHUNK // t
        return slice(c * per_chunk, (c + 1) * per_chunk), slice(None)

    def load_rows(ref, c, lanes):
        i0, i1 = chunk_index(c)
        return ref[i0, i1, lanes].reshape(CHUNK, -1)

    def store_rows(ref, c, lanes, val):
        i0, i1 = chunk_index(c)
        ref[i0, i1, lanes] = val if t >= CHUNK else val.reshape(CHUNK // t, t, -1)

    def state_index(c, jj):
        return c // (t // CHUNK) if t >= CHUNK else c * nseg + jj

    chains = []
    for c in range(nchunk):
        sc = load_rows(scal_ref, c, slice(None))
        gall = _fdot(incl_f, sc)
        gt = _fdot_nt(sel, gall)
        for hh in range(DN_HEADS):
            sl = slice(hh * DN_HEAD_DIM, (hh + 1) * DN_HEAD_DIM)
            chains.append(dict(
                c=c, hh=hh, sl=sl, beta=sc[:, hh:hh + 1],
                gc=gall[:, DN_HEADS + hh:DN_HEADS + hh + 1],
                gr=gt[DN_HEADS + hh:DN_HEADS + hh + 1, :]))
    yield
    for ch in chains:
        qh = load_rows(q_ref, ch["c"], ch["sl"])
        kh = load_rows(k_ref, ch["c"], ch["sl"])
        dec = jnp.where(incl, jnp.exp(jnp.where(incl, ch["gc"] - ch["gr"], 0.0)), 0.0)
        kb = kh * ch["beta"]
        kq = _bdot_nt(jnp.concatenate([kb, qh], axis=0), kh)
        ch.update(qh=qh, kh=kh, kb=kb,
                  x=jnp.where(strict, kq[:CHUNK] * dec, 0.0),
                  att=kq[CHUNK:] * dec)
    for ch in chains:
        ch["p"] = eye - ch["x"]
        if seg > 2:
            ch["x"] = _bdot(ch["x"], ch["x"])
    yield
    span = 4
    while span < seg:
        for ch in chains:
            both = _bdot(jnp.concatenate([ch["x"], ch["p"]], axis=0), ch["x"])
            ch["x"], ch["p"] = both[:CHUNK], ch["p"] + both[CHUNK:]
        yield
        span *= 2
    if seg > 2:
        for ch in chains:
            ch["p"] = ch["p"] + _bdot(ch["p"], ch["x"])
        yield
    for ch in chains:
        vh = load_rows(v_ref, ch["c"], ch["sl"])
        eg = jnp.exp(ch["gc"])
        sol = _bdot(ch["p"], jnp.concatenate([vh * ch["beta"], ch["kb"] * eg], axis=1))
        ch.update(u=sol[:, :DN_HEAD_DIM], w=sol[:, DN_HEAD_DIM:], qe=ch["qh"] * eg)
    yield
    for ch in chains:
        ch["s_old"], ch["rr"] = [], []
        for jj in range(nseg):
            rs = slice(jj * seg, (jj + 1) * seg)
            s_old = sout_ref[state_index(ch["c"], jj), ch["hh"]]
            ch["s_old"].append(s_old)
            ch["rr"].append(_bdot(jnp.concatenate([ch["w"][rs], ch["qe"][rs]], axis=0), s_old))
    yield
    for ch in chains:
        vnews = []
        for jj in range(nseg):
            rs = slice(jj * seg, (jj + 1) * seg)
            vnew = ch["u"][rs] - ch["rr"][jj][:seg]
            gl = ch["gc"][(jj + 1) * seg - 1:(jj + 1) * seg, :]
            kdec = ch["kh"][rs] * jnp.exp(gl - ch["gc"][rs])
            if nseg == 1:
                both = _bdot(jnp.concatenate([ch["att"], kdec.T], axis=0), vnew)
                ch["ov"], upd = both[:CHUNK], both[CHUNK:]
            else:
                upd = _bdot(kdec.T, vnew)
            sout_ref[state_index(ch["c"], jj), ch["hh"]] = ch["s_old"][jj] * jnp.exp(gl) + upd
            vnews.append(vnew)
        if nseg > 1:
            ch["ov"] = _bdot(ch["att"], jnp.concatenate(vnews, axis=0))
    yield
    for ch in chains:
        oint = ch["rr"][0][seg:] if nseg == 1 else jnp.concatenate([x[seg:] for x in ch["rr"]], axis=0)
        o = oint + ch["ov"]
        o = _rms(o, nw) * load_rows(zg_ref, ch["c"], ch["sl"])
        store_rows(mix_ref, ch["c"], slice(RG_WIDTH + ch["hh"] * DN_HEAD_DIM,
                                           RG_WIDTH + (ch["hh"] + 1) * DN_HEAD_DIM), o.astype(mix_ref.dtype))


def _kernel_a(x_ref, gpre1_ref, wup_ref, wdn_ref, gpost1_ref,
              gprem_ref, wmain_ref, wscal_ref, cw_ref, cb_ref, wrg_ref, brg_ref, lam_ref, alog_ref, dtb_ref,
              cst_ref, _prev_conv, *rest, tl):
    x1_ref, prep_ref, ncst_ref, hid_ref, xs_ref, cscr_ref = rest
    outs = _prep_views(prep_ref)
    g = pl.program_id(0)
    hist_rows = slice(CONV_PAD - (CONV_WIDTH - 1), CONV_PAD)

    @pl.when(g == 0)
    def _():
        xs_ref[...] = jnp.zeros_like(xs_ref)
        cscr_ref[...] = jnp.zeros_like(cscr_ref)

    @pl.when((g - 1) % tl.nt == 0)
    def _():
        cscr_ref[:, hist_rows, :] = cst_ref[...]

    def finish(x1):
        x1_ref[...] = x1.reshape(tl.ns, tl.t, D_MODEL)
        xs_ref[g % 2] = x1

    w = (gprem_ref, wmain_ref, wscal_ref, cw_ref, cb_ref, wrg_ref, brg_ref, lam_ref, alog_ref, dtb_ref)
    _interleave(
        _prep_steps(lambda: xs_ref[(g + 1) % 2], w, outs, ncst_ref, cscr_ref, tl.ns, tl.t),
        _ffn_steps(lambda: x_ref[...].reshape(tl.r, D_MODEL), gpre1_ref, wup_ref, wdn_ref, gpost1_ref, hid_ref,
                   finish))


def _call_a(x, layer, conv_state, prev_conv, w, *, ns, t):
    bn, l, _ = x.shape
    tl = _Tiling(bn, l, ns, t)
    f1, pw = w["f1"], w["prep"]
    big = {"wup", "wdn", "wmain", "wrg"}
    names = ["gpre", "wmain", "wscal", "cw", "cb", "wrg", "brg", "lam", "alog", "dtb"]
    args = [x, *f1] + [pw[n] for n in names] + [conv_state]
    in_specs = ([tl.rows(D_MODEL, "a"), _layer_spec(f1[0].shape, layer), _layer_spec(f1[1].shape, layer, single=True),
                 _layer_spec(f1[2].shape, layer, single=True), _layer_spec(f1[3].shape, layer)]
                + [_layer_spec(pw[n].shape, layer, single=n in big) for n in names]
                + [tl.state(conv_state.shape, layer, "b")])
    in_specs[5 + names.index("wmain")] = pl.BlockSpec(
        (None, D_MODEL, MAIN_COLS), lambda g: (layer, 0, 0), pipeline_mode=pl.Buffered(1))
    aliases = {len(args): 2}
    args.append(prev_conv)
    in_specs.append(pl.BlockSpec(memory_space=pl.ANY))
    return pl.pallas_call(
        functools.partial(_kernel_a, tl=tl),
        grid=(tl.n + 1,),
        in_specs=in_specs,
        out_specs=[tl.rows(D_MODEL, "a"), tl.rows(PREP_COLS, "b"), tl.state(conv_state.shape, layer, "b")],
        out_shape=[jax.ShapeDtypeStruct((bn, l, D_MODEL), F32), jax.ShapeDtypeStruct((bn, l, PREP_COLS), F32),
                   jax.ShapeDtypeStruct(conv_state.shape, F32)],
        scratch_shapes=[pltpu.VMEM((tl.r, D_FF), BF16), pltpu.VMEM((2, tl.r, D_MODEL), F32),
                        pltpu.VMEM((ns, CONV_PAD, CONV_CH), F32)],
        input_output_aliases=aliases,
        compiler_params=_cparams(),
        name="ffn1_prep",
    )(*args)


def _kernel_b(prep_ref, h0_ref, s0_ref, nw_ref, x1_ref, wout_ref, gmix_ref, _prev_h, _prev_s,
              xmid_ref, hout_ref, sout_ref, mix_ref, *, tl):
    g = pl.program_id(0)

    @pl.when(g % tl.nt == 0)
    def _():
        hout_ref[...] = h0_ref[...]
        sout_ref[...] = s0_ref[...]

    _interleave(_recur_steps(_prep_views(prep_ref), nw_ref, hout_ref, sout_ref, mix_ref, tl.ns, tl.t))
    m = mix_ref[...].reshape(tl.r, D_MODEL).astype(BF16)
    y = jnp.dot(m, wout_ref[...], preferred_element_type=F32)
    xm = x1_ref[...].reshape(tl.r, D_MODEL) + _rms(y, gmix_ref[...])
    xmid_ref[...] = xm.reshape(tl.ns, tl.t, D_MODEL)


def _call_b(prep, x1, layer, h0, s0, prev_h, prev_s, w, *, ns, t):
    bn, l, _ = prep.shape
    tl = _Tiling(bn, l, ns, t)
    args = [prep, h0, s0, w["nw"], x1, w["wout"], w["gmix"]]
    in_specs = [tl.rows(PREP_COLS, "a"), tl.state(h0.shape, layer, "a"), tl.state(s0.shape, layer, "a"),
                _layer_spec(w["nw"].shape, layer), tl.rows(D_MODEL, "a"),
                _layer_spec(w["wout"].shape, layer, single=True), _layer_spec(w["gmix"].shape, layer)]
    aliases = {len(args): 1, len(args) + 1: 2}
    args += [prev_h, prev_s]
    in_specs += [pl.BlockSpec(memory_space=pl.ANY)] * 2
    return pl.pallas_call(
        functools.partial(_kernel_b, tl=tl),
        grid=(tl.n,),
        in_specs=in_specs,
        out_specs=[tl.rows(D_MODEL, "a"), tl.state(h0.shape, layer, "a"), tl.state(s0.shape, layer, "a")],
        out_shape=[jax.ShapeDtypeStruct((bn, l, D_MODEL), F32), jax.ShapeDtypeStruct(h0.shape, F32),
                   jax.ShapeDtypeStruct(s0.shape, F32)],
        scratch_shapes=[pltpu.VMEM((ns, t, D_MODEL), F32)],
        input_output_aliases=aliases,
        compiler_params=_cparams(),
        name="recur_oproj",
    )(*args)


def _kernel_c(xp_ref, xs_ref, gpre_ref, wup_ref, wdn_ref, gpost_ref, *rest, nbp, final):
    rest = list(rest)
    gfin_ref = rest.pop(0) if final else None
    yp_ref, ys_ref, hid_ref = rest
    is_prompt = pl.program_id(0) < nbp

    def finish(out):
        if final:
            out = _rms(out, gfin_ref[...])

        @pl.when(is_prompt)
        def _():
            yp_ref[...] = out

        @pl.when(jnp.logical_not(is_prompt))
        def _():
            ys_ref[...] = out

    _interleave(_ffn_steps(lambda: jnp.where(is_prompt, xp_ref[...], xs_ref[...]),
                           gpre_ref, wup_ref, wdn_ref, gpost_ref, hid_ref, finish))


def _call_c(x_p, x_s, layer, w, gfin):
    n_p, n_s = x_p.shape[0], x_s.shape[0]
    nbp, nbs = n_p // TM, n_s // TM
    f2 = w["f2"]
    p_spec = pl.BlockSpec((TM, D_MODEL), lambda i: (jnp.minimum(i, nbp - 1), 0))
    s_spec = pl.BlockSpec((TM, D_MODEL), lambda i: (jnp.maximum(i - nbp, 0), 0))
    args = [x_p, x_s, *f2]
    in_specs = [p_spec, s_spec, _layer_spec(f2[0].shape, layer), _layer_spec(f2[1].shape, layer, single=True),
                _layer_spec(f2[2].shape, layer, single=True), _layer_spec(f2[3].shape, layer)]
    if gfin is not None:
        args.append(gfin)
        in_specs.append(pl.BlockSpec(gfin.shape, lambda i: (0, 0)))
    return pl.pallas_call(
        functools.partial(_kernel_c, nbp=nbp, final=gfin is not None),
        grid=(nbp + nbs,),
        in_specs=in_specs,
        out_specs=[p_spec, s_spec],
        out_shape=[jax.ShapeDtypeStruct((n_p, D_MODEL), F32), jax.ShapeDtypeStruct((n_s, D_MODEL), F32)],
        scratch_shapes=[pltpu.VMEM((TM, D_FF), BF16)],
        compiler_params=_cparams(),
        name="ffn2",
    )(*args)


def _lane_pad(mat, offset):
    depth, n = mat.shape
    out = jnp.zeros((depth, 1, LANES), F32)
    return lax.dynamic_update_slice(out, mat.reshape(depth, 1, n).astype(F32), (0, 0, offset))


def _weights(ffn1_norm_pre, ffn1_w_up, ffn1_w_down, ffn1_norm_post, mix_norm_pre, w_in, conv_w, conv_b_rg,
             rg_w_a, rg_b_a, rg_w_x, rg_b_x, rg_lambda, dn_a_log, dn_dt_bias, dn_norm_w, w_out, mix_norm_post,
             ffn2_norm_pre, ffn2_w_up, ffn2_w_down, ffn2_norm_post):
    depth = w_in.shape[0]
    row = lambda v: v.reshape(depth, 1, -1)
    wscal = jnp.zeros((depth, D_MODEL, LANES), BF16).at[:, :, :2 * DN_HEADS].set(
        w_in[:, :, MAIN_COLS:].astype(BF16))
    eye = jnp.eye(RG_BLOCKS, dtype=F32)
    bd = lambda w: jnp.einsum('lhij,hg->lhigj', w, eye).reshape(depth, RG_WIDTH, RG_WIDTH)
    return dict(
        f1=(row(ffn1_norm_pre), ffn1_w_up.astype(BF16), ffn1_w_down.astype(BF16), row(ffn1_norm_post)),
        f2=(row(ffn2_norm_pre), ffn2_w_up.astype(BF16), ffn2_w_down.astype(BF16), row(ffn2_norm_post)),
        prep=dict(gpre=row(mix_norm_pre), wmain=w_in.astype(BF16), wscal=wscal, cw=conv_w,
                  cb=row(conv_b_rg), wrg=jnp.concatenate([bd(rg_w_a), bd(rg_w_x)], axis=2).astype(BF16),
                  brg=jnp.concatenate([row(rg_b_a), row(rg_b_x)], axis=2), lam=row(rg_lambda),
                  alog=_lane_pad(dn_a_log, DN_HEADS), dtb=_lane_pad(dn_dt_bias, DN_HEADS)),
        nw=row(dn_norm_w),
        wout=w_out.astype(BF16),
        gmix=row(mix_norm_post),
    )


def kernel(x_prompt, x_sample, state_conv, state_rglru, state_delta, ffn1_norm_pre, ffn1_w_up, ffn1_w_down, ffn1_norm_post, mix_norm_pre, w_in, conv_w, conv_b_rg, rg_w_a, rg_b_a, rg_w_x, rg_b_x, rg_lambda, dn_a_log, dn_dt_bias, dn_norm_w, w_out, mix_norm_post, ffn2_norm_pre, ffn2_w_up, ffn2_w_down, ffn2_norm_post, final_norm):
    depth = w_in.shape[0]
    w = _weights(ffn1_norm_pre, ffn1_w_up, ffn1_w_down, ffn1_norm_post, mix_norm_pre, w_in, conv_w, conv_b_rg,
                 rg_w_a, rg_b_a, rg_w_x, rg_b_x, rg_lambda, dn_a_log, dn_dt_bias, dn_norm_w, w_out,
                 mix_norm_post, ffn2_norm_pre, ffn2_w_up, ffn2_w_down, ffn2_norm_post)
    bp, lp, _ = x_prompt.shape
    bs, ls, _ = x_sample.shape
    dt = x_prompt.dtype
    pops = [
        dict(x=x_prompt, blk_a=(4, CHUNK), blk_b=(bp, CHUNK),
             conv=jnp.zeros((depth, bp, CONV_WIDTH - 1, CONV_CH), dt),
             rg=jnp.zeros((depth, bp, 1, RG_WIDTH), dt),
             dn=jnp.zeros((depth, bp, DN_HEADS, DN_HEAD_DIM, DN_HEAD_DIM), dt)),
        dict(x=x_sample, blk_a=(4 * CHUNK // ls, ls), blk_b=(2 * CHUNK // ls, ls),
             conv=state_conv, rg=state_rglru.reshape(depth, bs, 1, RG_WIDTH), dn=state_delta),
    ]
    for p in pops:
        p.update(new_conv=jnp.zeros_like(p["conv"]), new_rg=jnp.zeros_like(p["rg"]),
                 new_dn=jnp.zeros_like(p["dn"]))
    pp, ps = pops
    flat = lambda v: v.reshape(-1, D_MODEL)
    for li in range(depth):
        for p in pops:
            p["x1"], prep, p["new_conv"] = _call_a(p["x"], li, p["conv"], p["new_conv"], w,
                                                   ns=p["blk_a"][0], t=p["blk_a"][1])
            p["xm"], p["new_rg"], p["new_dn"] = _call_b(prep, p["x1"], li, p["rg"], p["dn"], p["new_rg"],
                                                        p["new_dn"], w, ns=p["blk_b"][0], t=p["blk_b"][1])
        gfin = final_norm.reshape(1, -1) if li == depth - 1 else None
        y_p, y_s = _call_c(flat(pp["xm"]), flat(ps["xm"]), li, w, gfin)
        pp["x"], ps["x"] = y_p.reshape(pp["x"].shape), y_s.reshape(ps["x"].shape)
    return (pp["x"], ps["x"], pp["new_conv"], pp["new_rg"].reshape(depth, bp, RG_WIDTH), pp["new_dn"],
            ps["new_conv"], ps["new_rg"].reshape(depth, bs, RG_WIDTH), ps["new_dn"])
```

```python
import functools

import jax
import jax.numpy as jnp
from jax import lax
from jax.experimental import pallas as pl
from jax.experimental.pallas import tpu as pltpu

F32 = jnp.float32
BF16 = jnp.bfloat16

D_MODEL = 1024
D_FF = 2816
RG_WIDTH = 512
RG_BLOCKS = 8
RG_BLOCK = RG_WIDTH // RG_BLOCKS
RG_C = 8.0
DN_HEADS = 4
DN_HEAD_DIM = 128
DN_WIDTH = DN_HEADS * DN_HEAD_DIM
CONV_WIDTH = 4
CONV_CH = RG_WIDTH + 3 * DN_WIDTH
GATE_COLS = RG_WIDTH + DN_WIDTH
MAIN_COLS = CONV_CH + GATE_COLS
EPS = 1e-6

LANES = 128
SUBLANES = 8
CHUNK = 64
CONV_PAD = SUBLANES
FF_TILE = 256
TM = 512
VMEM_LIMIT = 56 * 1024 * 1024
PREP_WIDTHS = (RG_WIDTH, RG_WIDTH, RG_WIDTH, DN_WIDTH, DN_WIDTH, DN_WIDTH, DN_WIDTH, LANES)
PREP_COLS = sum(PREP_WIDTHS)


def _prep_views(prep_ref):
    views, off = [], 0
    for width in PREP_WIDTHS:
        views.append(prep_ref.at[:, :, off:off + width])
        off += width
    return tuple(views)


def _rms(x, g):
    return x * lax.rsqrt(jnp.mean(x * x, axis=-1, keepdims=True) + EPS) * g


def _sigmoid(x):
    return 0.5 + 0.5 * jnp.tanh(0.5 * x)


def _silu(x):
    hx = 0.5 * x
    return hx + hx * jnp.tanh(hx)


def _softplus(x):
    return jnp.maximum(x, 0.0) + jnp.log1p(jnp.exp(-jnp.abs(x)))


def _bdot(a, b):
    return jnp.dot(a.astype(BF16), b.astype(BF16), preferred_element_type=F32)


def _bdot_nt(a, b):
    return lax.dot_general(a.astype(BF16), b.astype(BF16), (((1,), (1,)), ((), ())),
                           preferred_element_type=F32)


def _fdot(a, b):
    return jnp.dot(a, b, preferred_element_type=F32, precision=lax.Precision.HIGHEST)


def _fdot_nt(a, b):
    return lax.dot_general(a, b, (((1,), (1,)), ((), ())), preferred_element_type=F32,
                           precision=lax.Precision.HIGHEST)


def _layer_spec(shape, layer, *, single=False):
    nd = len(shape)
    kw = dict(pipeline_mode=pl.Buffered(1)) if single else {}
    return pl.BlockSpec((None,) + tuple(shape[1:]), lambda *_: (layer,) + (0,) * (nd - 1), **kw)


class _Tiling:
    def __init__(self, bn, l, ns, t):
        assert bn % ns == 0 and l % t == 0
        self.ns, self.t, self.r = ns, t, ns * t
        self.nt = l // t
        self.n = (bn // ns) * self.nt

    def tile_a(self, g):
        return jnp.minimum(g, self.n - 1)

    def tile_b(self, g):
        return jnp.maximum(g - 1, 0)

    def rows(self, width, stage):
        tile = self.tile_a if stage == "a" else self.tile_b
        return pl.BlockSpec((self.ns, self.t, width),
                            lambda g: (tile(g) // self.nt, tile(g) % self.nt, 0))

    def state(self, shape, layer, stage):
        tile = self.tile_a if stage == "a" else self.tile_b
        nd = len(shape)
        return pl.BlockSpec((None, self.ns) + tuple(shape[2:]),
                            lambda g: (layer, tile(g) // self.nt) + (0,) * (nd - 2))


def _cparams():
    return pltpu.CompilerParams(dimension_semantics=("arbitrary",), vmem_limit_bytes=VMEM_LIMIT)


def _interleave(*stages):
    live = list(stages)
    while live:
        for st in list(live):
            try:
                next(st)
            except StopIteration:
                live.remove(st)


def _ffn_steps(x_fn, gpre_ref, wup_ref, wdn_ref, gpost_ref, hid_ref, finish):
    x = x_fn()
    h = _rms(x, gpre_ref[...]).astype(BF16)
    yield
    for j in range(D_FF // FF_TILE):
        g = jnp.dot(h, wup_ref[:, j * FF_TILE:(j + 1) * FF_TILE], preferred_element_type=F32)
        u = jnp.dot(h, wup_ref[:, D_FF + j * FF_TILE:D_FF + (j + 1) * FF_TILE], preferred_element_type=F32)
        hid_ref[:, j * FF_TILE:(j + 1) * FF_TILE] = (_silu(g) * u).astype(BF16)
        yield
    ys = []
    for c in range(D_MODEL // FF_TILE):
        ys.append(jnp.dot(hid_ref[...], wdn_ref[:, c * FF_TILE:(c + 1) * FF_TILE], preferred_element_type=F32))
        yield
    finish(x + 0.5 * _rms(jnp.concatenate(ys, axis=1), gpost_ref[...]))


def _prep_steps(x_fn, w, outs, ncst_ref, cscr_ref, ns, t):
    gpre_ref, wmain_ref, wscal_ref, cw_ref, cb_ref, wrg_ref, brg_ref, lam_ref, alog_ref, dtb_ref = w
    a_ref, b_ref, rgate_ref, q_ref, k_ref, v_ref, zg_ref, scal_ref = outs
    r = ns * t
    cw_n = FF_TILE
    h = _rms(x_fn(), gpre_ref[...]).astype(BF16)
    yield

    def proj_cols(c0):
        return jnp.dot(h, wmain_ref[:, c0:c0 + cw_n], preferred_element_type=F32)

    def conv_cols(c0):
        cols = slice(c0, c0 + cw_n)
        x = proj_cols(c0).reshape(ns, t, cw_n)
        hist = cscr_ref[:, :, cols]
        taps = [cw_ref[j:j + 1, cols] for j in range(CONV_WIDTH)]
        u, hu = taps[0] * x, taps[0] * hist
        for j in range(1, CONV_WIDTH):
            sh = pltpu.roll(jnp.concatenate([hu, u], axis=1), 1, axis=1)
            u, hu = taps[j] * x + sh[:, CONV_PAD:], taps[j] * hist + sh[:, :CONV_PAD]
        new_hist = x[:, t - (CONV_WIDTH - 1):, :]
        ncst_ref[:, :, cols] = new_hist
        cscr_ref[:, CONV_PAD - (CONV_WIDTH - 1):, cols] = new_hist
        return u.reshape(r, cw_n)

    xr_parts = []
    for c0 in range(0, RG_WIDTH, cw_n):
        xr_parts.append(conv_cols(c0) + cb_ref[:, c0:c0 + cw_n])
        yield
    xr = jnp.concatenate(xr_parts, axis=1)
    gax = jnp.dot(xr.astype(BF16), wrg_ref[...], preferred_element_type=F32) + brg_ref[...]
    rr = _sigmoid(gax[:, :RG_WIDTH])
    ii = _sigmoid(gax[:, RG_WIDTH:])
    log_a = -RG_C * rr * _softplus(-lam_ref[...])
    a = jnp.exp(log_a)
    a_ref[...] = a.reshape(ns, t, RG_WIDTH)
    one_minus_a2 = -jnp.tanh(log_a) * (a * a + 1.0)
    b_ref[...] = (jnp.sqrt(one_minus_a2) * (ii * xr)).reshape(ns, t, RG_WIDTH)
    yield

    for base, ref, scale in ((RG_WIDTH, q_ref, DN_HEAD_DIM ** -0.5), (RG_WIDTH + DN_WIDTH, k_ref, None)):
        for c0 in range(0, DN_WIDTH, cw_n):
            act = _silu(conv_cols(base + c0))
            for hh in range(cw_n // DN_HEAD_DIM):
                xh = act[:, hh * DN_HEAD_DIM:(hh + 1) * DN_HEAD_DIM]
                inv = lax.rsqrt(jnp.sum(xh * xh, axis=-1, keepdims=True) + EPS)
                xn = xh * (inv if scale is None else inv * scale)
                lo = c0 + hh * DN_HEAD_DIM
                ref[:, :, lo:lo + DN_HEAD_DIM] = xn.reshape(ns, t, DN_HEAD_DIM)
            yield
    for c0 in range(0, DN_WIDTH, cw_n):
        v_ref[:, :, c0:c0 + cw_n] = _silu(conv_cols(RG_WIDTH + 2 * DN_WIDTH + c0)).reshape(ns, t, cw_n)
        yield

    for c0 in range(0, RG_WIDTH, cw_n):
        rgate_ref[:, :, c0:c0 + cw_n] = jax.nn.gelu(proj_cols(CONV_CH + c0)).reshape(ns, t, cw_n)
        yield
    for c0 in range(0, DN_WIDTH, cw_n):
        zg_ref[:, :, c0:c0 + cw_n] = _silu(proj_cols(CONV_CH + RG_WIDTH + c0)).reshape(ns, t, cw_n)
        yield

    scal = jnp.dot(h, wscal_ref[...], preferred_element_type=F32)
    lane = lax.broadcasted_iota(jnp.int32, scal.shape, 1)
    beta = _sigmoid(scal)
    g = -jnp.exp(alog_ref[...]) * _softplus(scal + dtb_ref[...])
    scal_ref[...] = jnp.where(lane < DN_HEADS, beta, g).reshape(ns, t, LANES)


def _recur_steps(ins, nw_ref, hout_ref, sout_ref, mix_ref, ns, t):
    a_ref, b_ref, rgate_ref, q_ref, k_ref, v_ref, zg_ref, scal_ref = ins
    r = ns * t
    seg = min(CHUNK, t)
    nseg = CHUNK // seg
    nchunk = r // CHUNK

    rowmod = lax.broadcasted_iota(jnp.int32, (t // SUBLANES, SUBLANES, RG_WIDTH), 1)

    def rg_seq(si, carry):
        a = a_ref[si].reshape(t // SUBLANES, SUBLANES, RG_WIDTH)
        b = b_ref[si].reshape(t // SUBLANES, SUBLANES, RG_WIDTH)
        for s in (1, 2, 4):
            m = rowmod >= s
            a_sh = pltpu.roll(a, s, axis=1)
            b_sh = pltpu.roll(b, s, axis=1)
            b = jnp.where(m, a * b_sh + b, b)
            a = jnp.where(m, a * a_sh, a)
        hc = hout_ref[si]
        for gi in range(t // SUBLANES):
            rows = slice(gi * SUBLANES, (gi + 1) * SUBLANES)
            hb = a[gi] * hc + b[gi]
            mix_ref[si, rows, :RG_WIDTH] = (hb * rgate_ref[si, rows, :]).astype(mix_ref.dtype)
            hc = hb[SUBLANES - 1:SUBLANES, :]
        hout_ref[si] = hc
        return carry

    lax.fori_loop(0, ns, rg_seq, 0)
    yield

    ri = lax.broadcasted_iota(jnp.int32, (CHUNK, CHUNK), 0)
    ci = lax.broadcasted_iota(jnp.int32, (CHUNK, CHUNK), 1)
    seg_shift = seg.bit_length() - 1
    same = (ri >> seg_shift) == (ci >> seg_shift)
    incl = jnp.logical_and(same, ri >= ci)
    strict = jnp.logical_and(same, ri > ci)
    incl_f = incl.astype(F32)
    eye = (ri == ci).astype(F32)
    sel = (lax.broadcasted_iota(jnp.int32, (SUBLANES, LANES), 0)
           == lax.broadcasted_iota(jnp.int32, (SUBLANES, LANES), 1)).astype(F32)
    nw = nw_ref[...]

    def chunk_index(c):
        if t >= CHUNK:
            per_seq = t // CHUNK
            return c // per_seq, slice((c % per_seq) * CHUNK, (c % per_seq + 1) * CHUNK)
        per_chunk = CHUNK // t
        return slice(c * per_chunk, (c + 1) * per_chunk), slice(None)

    def load_rows(ref, c, lanes):
        i0, i1 = chunk_index(c)
        return ref[i0, i1, lanes].reshape(CHUNK, -1)

    def store_rows(ref, c, lanes, val):
        i0, i1 = chunk_index(c)
        ref[i0, i1, lanes] = val if t >= CHUNK else val.reshape(CHUNK // t, t, -1)

    def state_index(c, jj):
        return c // (t // CHUNK) if t >= CHUNK else c * nseg + jj

    chains = []
    for c in range(nchunk):
        sc = load_rows(scal_ref, c, slice(None))
        gall = _fdot(incl_f, sc)
        gt = _fdot_nt(sel, gall)
        for hh in range(DN_HEADS):
            sl = slice(hh * DN_HEAD_DIM, (hh + 1) * DN_HEAD_DIM)
            chains.append(dict(
                c=c, hh=hh, sl=sl, beta=sc[:, hh:hh + 1],
                gc=gall[:, DN_HEADS + hh:DN_HEADS + hh + 1],
                gr=gt[DN_HEADS + hh:DN_HEADS + hh + 1, :]))
    yield
    for ch in chains:
        qh = load_rows(q_ref, ch["c"], ch["sl"])
        kh = load_rows(k_ref, ch["c"], ch["sl"])
        dec = jnp.where(incl, jnp.exp(jnp.where(incl, ch["gc"] - ch["gr"], 0.0)), 0.0)
        kb = kh * ch["beta"]
        kq = _bdot_nt(jnp.concatenate([kb, qh], axis=0), kh)
        ch.update(qh=qh, kh=kh, kb=kb,
                  x=jnp.where(strict, kq[:CHUNK] * dec, 0.0),
                  att=kq[CHUNK:] * dec)
    for ch in chains:
        ch["p"] = eye - ch["x"]
        if seg > 2:
            ch["x"] = _bdot(ch["x"], ch["x"])
    yield
    span = 4
    while span < seg:
        for ch in chains:
            both = _bdot(jnp.concatenate([ch["x"], ch["p"]], axis=0), ch["x"])
            ch["x"], ch["p"] = both[:CHUNK], ch["p"] + both[CHUNK:]
        yield
        span *= 2
    if seg > 2:
        for ch in chains:
            ch["p"] = ch["p"] + _bdot(ch["p"], ch["x"])
        yield
    for ch in chains:
        vh = load_rows(v_ref, ch["c"], ch["sl"])
        eg = jnp.exp(ch["gc"])
        sol = _bdot(ch["p"], jnp.concatenate([vh * ch["beta"], ch["kb"] * eg], axis=1))
        ch.update(u=sol[:, :DN_HEAD_DIM], w=sol[:, DN_HEAD_DIM:], qe=ch["qh"] * eg)
    yield
    for ch in chains:
        ch["s_old"], ch["rr"] = [], []
        for jj in range(nseg):
            rs = slice(jj * seg, (jj + 1) * seg)
            s_old = sout_ref[state_index(ch["c"], jj), ch["hh"]]
            ch["s_old"].append(s_old)
            ch["rr"].append(_bdot(jnp.concatenate([ch["w"][rs], ch["qe"][rs]], axis=0), s_old))
    yield
    for ch in chains:
        vnews = []
        for jj in range(nseg):
            rs = slice(jj * seg, (jj + 1) * seg)
            vnew = ch["u"][rs] - ch["rr"][jj][:seg]
            gl = ch["gc"][(jj + 1) * seg - 1:(jj + 1) * seg, :]
            kdec = ch["kh"][rs] * jnp.exp(gl - ch["gc"][rs])
            if nseg == 1:
                both = _bdot(jnp.concatenate([ch["att"], kdec.T], axis=0), vnew)
                ch["ov"], upd = both[:CHUNK], both[CHUNK:]
            else:
                upd = _bdot(kdec.T, vnew)
            sout_ref[state_index(ch["c"], jj), ch["hh"]] = ch["s_old"][jj] * jnp.exp(gl) + upd
            vnews.append(vnew)
        if nseg > 1:
            ch["ov"] = _bdot(ch["att"], jnp.concatenate(vnews, axis=0))
    yield
    for ch in chains:
        oint = ch["rr"][0][seg:] if nseg == 1 else jnp.concatenate([x[seg:] for x in ch["rr"]], axis=0)
        o = oint + ch["ov"]
        o = _rms(o, nw) * load_rows(zg_ref, ch["c"], ch["sl"])
        store_rows(mix_ref, ch["c"], slice(RG_WIDTH + ch["hh"] * DN_HEAD_DIM,
                                           RG_WIDTH + (ch["hh"] + 1) * DN_HEAD_DIM), o.astype(mix_ref.dtype))


def _kernel_a(x_ref, gpre1_ref, wup_ref, wdn_ref, gpost1_ref,
              gprem_ref, wmain_ref, wscal_ref, cw_ref, cb_ref, wrg_ref, brg_ref, lam_ref, alog_ref, dtb_ref,
              cst_ref, _prev_conv, *rest, tl):
    x1_ref, prep_ref, ncst_ref, hid_ref, xs_ref, cscr_ref = rest
    outs = _prep_views(prep_ref)
    g = pl.program_id(0)
    hist_rows = slice(CONV_PAD - (CONV_WIDTH - 1), CONV_PAD)

    @pl.when(g == 0)
    def _():
        xs_ref[...] = jnp.zeros_like(xs_ref)
        cscr_ref[...] = jnp.zeros_like(cscr_ref)

    @pl.when((g - 1) % tl.nt == 0)
    def _():
        cscr_ref[:, hist_rows, :] = cst_ref[...]

    def finish(x1):
        x1_ref[...] = x1.reshape(tl.ns, tl.t, D_MODEL)
        xs_ref[g % 2] = x1

    w = (gprem_ref, wmain_ref, wscal_ref, cw_ref, cb_ref, wrg_ref, brg_ref, lam_ref, alog_ref, dtb_ref)
    _interleave(
        _prep_steps(lambda: xs_ref[(g + 1) % 2], w, outs, ncst_ref, cscr_ref, tl.ns, tl.t),
        _ffn_steps(lambda: x_ref[...].reshape(tl.r, D_MODEL), gpre1_ref, wup_ref, wdn_ref, gpost1_ref, hid_ref,
                   finish))


def _call_a(x, layer, conv_state, prev_conv, w, *, ns, t):
    bn, l, _ = x.shape
    tl = _Tiling(bn, l, ns, t)
    f1, pw = w["f1"], w["prep"]
    big = {"wup", "wdn", "wmain", "wrg"}
    names = ["gpre", "wmain", "wscal", "cw", "cb", "wrg", "brg", "lam", "alog", "dtb"]
    args = [x, *f1] + [pw[n] for n in names] + [conv_state]
    in_specs = ([tl.rows(D_MODEL, "a"), _layer_spec(f1[0].shape, layer), _layer_spec(f1[1].shape, layer, single=True),
                 _layer_spec(f1[2].shape, layer, single=True), _layer_spec(f1[3].shape, layer)]
                + [_layer_spec(pw[n].shape, layer, single=n in big) for n in names]
                + [tl.state(conv_state.shape, layer, "b")])
    in_specs[5 + names.index("wmain")] = pl.BlockSpec(
        (None, D_MODEL, MAIN_COLS), lambda g: (layer, 0, 0), pipeline_mode=pl.Buffered(1))
    aliases = {len(args): 2}
    args.append(prev_conv)
    in_specs.append(pl.BlockSpec(memory_space=pl.ANY))
    return pl.pallas_call(
        functools.partial(_kernel_a, tl=tl),
        grid=(tl.n + 1,),
        in_specs=in_specs,
        out_specs=[tl.rows(D_MODEL, "a"), tl.rows(PREP_COLS, "b"), tl.state(conv_state.shape, layer, "b")],
        out_shape=[jax.ShapeDtypeStruct((bn, l, D_MODEL), F32), jax.ShapeDtypeStruct((bn, l, PREP_COLS), F32),
                   jax.ShapeDtypeStruct(conv_state.shape, F32)],
        scratch_shapes=[pltpu.VMEM((tl.r, D_FF), BF16), pltpu.VMEM((2, tl.r, D_MODEL), F32),
                        pltpu.VMEM((ns, CONV_PAD, CONV_CH), F32)],
        input_output_aliases=aliases,
        compiler_params=_cparams(),
        name="ffn1_prep",
    )(*args)


def _kernel_b(prep_ref, h0_ref, s0_ref, nw_ref, x1_ref, wout_ref, gmix_ref, _prev_h, _prev_s,
              xmid_ref, hout_ref, sout_ref, mix_ref, *, tl):
    g = pl.program_id(0)

    @pl.when(g % tl.nt == 0)
    def _():
        hout_ref[...] = h0_ref[...]
        sout_ref[...] = s0_ref[...]

    _interleave(_recur_steps(_prep_views(prep_ref), nw_ref, hout_ref, sout_ref, mix_ref, tl.ns, tl.t))
    m = mix_ref[...].reshape(tl.r, D_MODEL).astype(BF16)
    y = jnp.dot(m, wout_ref[...], preferred_element_type=F32)
    xm = x1_ref[...].reshape(tl.r, D_MODEL) + _rms(y, gmix_ref[...])
    xmid_ref[...] = xm.reshape(tl.ns, tl.t, D_MODEL)


def _call_b(prep, x1, layer, h0, s0, prev_h, prev_s, w, *, ns, t):
    bn, l, _ = prep.shape
    tl = _Tiling(bn, l, ns, t)
    args = [prep, h0, s0, w["nw"], x1, w["wout"], w["gmix"]]
    in_specs = [tl.rows(PREP_COLS, "a"), tl.state(h0.shape, layer, "a"), tl.state(s0.shape, layer, "a"),
                _layer_spec(w["nw"].shape, layer), tl.rows(D_MODEL, "a"),
                _layer_spec(w["wout"].shape, layer, single=True), _layer_spec(w["gmix"].shape, layer)]
    aliases = {len(args): 1, len(args) + 1: 2}
    args += [prev_h, prev_s]
    in_specs += [pl.BlockSpec(memory_space=pl.ANY)] * 2
    return pl.pallas_call(
        functools.partial(_kernel_b, tl=tl),
        grid=(tl.n,),
        in_specs=in_specs,
        out_specs=[tl.rows(D_MODEL, "a"), tl.state(h0.shape, layer, "a"), tl.state(s0.shape, layer, "a")],
        out_shape=[jax.ShapeDtypeStruct((bn, l, D_MODEL), F32), jax.ShapeDtypeStruct(h0.shape, F32),
                   jax.ShapeDtypeStruct(s0.shape, F32)],
        scratch_shapes=[pltpu.VMEM((ns, t, D_MODEL), F32)],
        input_output_aliases=aliases,
        compiler_params=_cparams(),
        name="recur_oproj",
    )(*args)


def _kernel_c(xp_ref, xs_ref, gpre_ref, wup_ref, wdn_ref, gpost_ref, *rest, nbp, final):
    rest = list(rest)
    gfin_ref = rest.pop(0) if final else None
    yp_ref, ys_ref, hid_ref = rest
    is_prompt = pl.program_id(0) < nbp

    def finish(out):
        if final:
            out = _rms(out, gfin_ref[...])

        @pl.when(is_prompt)
        def _():
            yp_ref[...] = out

        @pl.when(jnp.logical_not(is_prompt))
        def _():
            ys_ref[...] = out

    _interleave(_ffn_steps(lambda: jnp.where(is_prompt, xp_ref[...], xs_ref[...]),
                           gpre_ref, wup_ref, wdn_ref, gpost_ref, hid_ref, finish))


def _call_c(x_p, x_s, layer, w, gfin):
    n_p, n_s = x_p.shape[0], x_s.shape[0]
    nbp, nbs = n_p // TM, n_s // TM
    f2 = w["f2"]
    p_spec = pl.BlockSpec((TM, D_MODEL), lambda i: (jnp.minimum(i, nbp - 1), 0))
    s_spec = pl.BlockSpec((TM, D_MODEL), lambda i: (jnp.maximum(i - nbp, 0), 0))
    args = [x_p, x_s, *f2]
    in_specs = [p_spec, s_spec, _layer_spec(f2[0].shape, layer), _layer_spec(f2[1].shape, layer, single=True),
                _layer_spec(f2[2].shape, layer, single=True), _layer_spec(f2[3].shape, layer)]
    if gfin is not None:
        args.append(gfin)
        in_specs.append(pl.BlockSpec(gfin.shape, lambda i: (0, 0)))
    return pl.pallas_call(
        functools.partial(_kernel_c, nbp=nbp, final=gfin is not None),
        grid=(nbp + nbs,),
        in_specs=in_specs,
        out_specs=[p_spec, s_spec],
        out_shape=[jax.ShapeDtypeStruct((n_p, D_MODEL), F32), jax.ShapeDtypeStruct((n_s, D_MODEL), F32)],
        scratch_shapes=[pltpu.VMEM((TM, D_FF), BF16)],
        compiler_params=_cparams(),
        name="ffn2",
    )(*args)


def _lane_pad(mat, offset):
    depth, n = mat.shape
    out = jnp.zeros((depth, 1, LANES), F32)
    return lax.dynamic_update_slice(out, mat.reshape(depth, 1, n).astype(F32), (0, 0, offset))


def _weights(ffn1_norm_pre, ffn1_w_up, ffn1_w_down, ffn1_norm_post, mix_norm_pre, w_in, conv_w, conv_b_rg,
             rg_w_a, rg_b_a, rg_w_x, rg_b_x, rg_lambda, dn_a_log, dn_dt_bias, dn_norm_w, w_out, mix_norm_post,
             ffn2_norm_pre, ffn2_w_up, ffn2_w_down, ffn2_norm_post):
    depth = w_in.shape[0]
    row = lambda v: v.reshape(depth, 1, -1)
    wscal = jnp.zeros((depth, D_MODEL, LANES), BF16).at[:, :, :2 * DN_HEADS].set(
        w_in[:, :, MAIN_COLS:].astype(BF16))
    eye = jnp.eye(RG_BLOCKS, dtype=F32)
    bd = lambda w: jnp.einsum('lhij,hg->lhigj', w, eye).reshape(depth, RG_WIDTH, RG_WIDTH)
    return dict(
        f1=(row(ffn1_norm_pre), ffn1_w_up.astype(BF16), ffn1_w_down.astype(BF16), row(ffn1_norm_post)),
        f2=(row(ffn2_norm_pre), ffn2_w_up.astype(BF16), ffn2_w_down.astype(BF16), row(ffn2_norm_post)),
        prep=dict(gpre=row(mix_norm_pre), wmain=w_in.astype(BF16), wscal=wscal, cw=conv_w,
                  cb=row(conv_b_rg), wrg=jnp.concatenate([bd(rg_w_a), bd(rg_w_x)], axis=2).astype(BF16),
                  brg=jnp.concatenate([row(rg_b_a), row(rg_b_x)], axis=2), lam=row(rg_lambda),
                  alog=_lane_pad(dn_a_log, DN_HEADS), dtb=_lane_pad(dn_dt_bias, DN_HEADS)),
        nw=row(dn_norm_w),
        wout=w_out.astype(BF16),
        gmix=row(mix_norm_post),
    )


def kernel(x_prompt, x_sample, state_conv, state_rglru, state_delta, ffn1_norm_pre, ffn1_w_up, ffn1_w_down, ffn1_norm_post, mix_norm_pre, w_in, conv_w, conv_b_rg, rg_w_a, rg_b_a, rg_w_x, rg_b_x, rg_lambda, dn_a_log, dn_dt_bias, dn_norm_w, w_out, mix_norm_post, ffn2_norm_pre, ffn2_w_up, ffn2_w_down, ffn2_norm_post, final_norm):
    depth = w_in.shape[0]
    w = _weights(ffn1_norm_pre, ffn1_w_up, ffn1_w_down, ffn1_norm_post, mix_norm_pre, w_in, conv_w, conv_b_rg,
                 rg_w_a, rg_b_a, rg_w_x, rg_b_x, rg_lambda, dn_a_log, dn_dt_bias, dn_norm_w, w_out,
                 mix_norm_post, ffn2_norm_pre, ffn2_w_up, ffn2_w_down, ffn2_norm_post)
    bp, lp, _ = x_prompt.shape
    bs, ls, _ = x_sample.shape
    dt = x_prompt.dtype
    pops = [
        dict(x=x_prompt, blk_a=(1, 4 * CHUNK), blk_b=(bp, CHUNK),
             conv=jnp.zeros((depth, bp, CONV_WIDTH - 1, CONV_CH), dt),
             rg=jnp.zeros((depth, bp, 1, RG_WIDTH), dt),
             dn=jnp.zeros((depth, bp, DN_HEADS, DN_HEAD_DIM, DN_HEAD_DIM), dt)),
        dict(x=x_sample, blk_a=(4 * CHUNK // ls, ls), blk_b=(2 * CHUNK // ls, ls),
             conv=state_conv, rg=state_rglru.reshape(depth, bs, 1, RG_WIDTH), dn=state_delta),
    ]
    for p in pops:
        p.update(new_conv=jnp.zeros_like(p["conv"]), new_rg=jnp.zeros_like(p["rg"]),
                 new_dn=jnp.zeros_like(p["dn"]))
    pp, ps = pops
    flat = lambda v: v.reshape(-1, D_MODEL)
    for li in range(depth):
        for p in pops:
            p["x1"], prep, p["new_conv"] = _call_a(p["x"], li, p["conv"], p["new_conv"], w,
                                                   ns=p["blk_a"][0], t=p["blk_a"][1])
            p["xm"], p["new_rg"], p["new_dn"] = _call_b(prep, p["x1"], li, p["rg"], p["dn"], p["new_rg"],
                                                        p["new_dn"], w, ns=p["blk_b"][0], t=p["blk_b"][1])
        gfin = final_norm.reshape(1, -1) if li == depth - 1 else None
        y_p, y_s = _call_c(flat(pp["xm"]), flat(ps["xm"]), li, w, gfin)
        pp["x"], ps["x"] = y_p.reshape(pp["x"].shape), y_s.reshape(ps["x"].shape)
    return (pp["x"], ps["x"], pp["new_conv"], pp["new_rg"].reshape(depth, bp, RG_WIDTH), pp["new_dn"],
            ps["new_conv"], ps["new_rg"].reshape(depth, bs, RG_WIDTH), ps["new_dn"])
```

```python
import functools

import jax
import jax.numpy as jnp
from jax import lax
from jax.experimental import pallas as pl
from jax.experimental.pallas import tpu as pltpu

F32 = jnp.float32
BF16 = jnp.bfloat16

D_MODEL = 1024
D_FF = 2816
RG_WIDTH = 512
RG_BLOCKS = 8
RG_BLOCK = RG_WIDTH // RG_BLOCKS
RG_C = 8.0
DN_HEADS = 4
DN_HEAD_DIM = 128
DN_WIDTH = DN_HEADS * DN_HEAD_DIM
CONV_WIDTH = 4
CONV_CH = RG_WIDTH + 3 * DN_WIDTH
GATE_COLS = RG_WIDTH + DN_WIDTH
MAIN_COLS = CONV_CH + GATE_COLS
EPS = 1e-6

LANES = 128
SUBLANES = 8
CHUNK = 64
CONV_PAD = SUBLANES
FF_TILE = 256
TM = 512
VMEM_LIMIT = 56 * 1024 * 1024
PREP_WIDTHS = (RG_WIDTH, RG_WIDTH, RG_WIDTH, DN_WIDTH, DN_WIDTH, DN_WIDTH, DN_WIDTH, LANES)
PREP_COLS = sum(PREP_WIDTHS)


def _prep_views(prep_ref):
    views, off = [], 0
    for width in PREP_WIDTHS:
        views.append(prep_ref.at[:, :, off:off + width])
        off += width
    return tuple(views)


def _rms(x, g):
    return x * lax.rsqrt(jnp.mean(x * x, axis=-1, keepdims=True) + EPS) * g


def _sigmoid(x):
    return 0.5 + 0.5 * jnp.tanh(0.5 * x)


def _silu(x):
    hx = 0.5 * x
    return hx + hx * jnp.tanh(hx)


def _softplus(x):
    return jnp.maximum(x, 0.0) + jnp.log1p(jnp.exp(-jnp.abs(x)))


def _bdot(a, b):
    return jnp.dot(a.astype(BF16), b.astype(BF16), preferred_element_type=F32)


def _bdot_nt(a, b):
    return lax.dot_general(a.astype(BF16), b.astype(BF16), (((1,), (1,)), ((), ())),
                           preferred_element_type=F32)


def _fdot(a, b):
    return jnp.dot(a, b, preferred_element_type=F32, precision=lax.Precision.HIGHEST)


def _fdot_nt(a, b):
    return lax.dot_general(a, b, (((1,), (1,)), ((), ())), preferred_element_type=F32,
                           precision=lax.Precision.HIGHEST)


def _layer_spec(shape, layer, *, single=False):
    nd = len(shape)
    kw = dict(pipeline_mode=pl.Buffered(1)) if single else {}
    return pl.BlockSpec((None,) + tuple(shape[1:]), lambda *_: (layer,) + (0,) * (nd - 1), **kw)


class _Tiling:
    def __init__(self, bn, l, ns, t):
        assert bn % ns == 0 and l % t == 0
        self.ns, self.t, self.r = ns, t, ns * t
        self.nt = l // t
        self.n = (bn // ns) * self.nt

    def tile_a(self, g):
        return jnp.minimum(g, self.n - 1)

    def tile_b(self, g):
        return jnp.maximum(g - 1, 0)

    def rows(self, width, stage):
        tile = self.tile_a if stage == "a" else self.tile_b
        return pl.BlockSpec((self.ns, self.t, width),
                            lambda g: (tile(g) // self.nt, tile(g) % self.nt, 0))

    def state(self, shape, layer, stage):
        tile = self.tile_a if stage == "a" else self.tile_b
        nd = len(shape)
        return pl.BlockSpec((None, self.ns) + tuple(shape[2:]),
                            lambda g: (layer, tile(g) // self.nt) + (0,) * (nd - 2))


def _cparams():
    return pltpu.CompilerParams(dimension_semantics=("arbitrary",), vmem_limit_bytes=VMEM_LIMIT)


def _interleave(*stages):
    live = list(stages)
    while live:
        for st in list(live):
            try:
                next(st)
            except StopIteration:
                live.remove(st)


def _ffn_steps(x_fn, gpre_ref, wup_ref, wdn_ref, gpost_ref, hid_ref, finish):
    x = x_fn()
    h = _rms(x, gpre_ref[...]).astype(BF16)
    yield
    for j in range(D_FF // FF_TILE):
        g = jnp.dot(h, wup_ref[:, j * FF_TILE:(j + 1) * FF_TILE], preferred_element_type=F32)
        u = jnp.dot(h, wup_ref[:, D_FF + j * FF_TILE:D_FF + (j + 1) * FF_TILE], preferred_element_type=F32)
        hid_ref[:, j * FF_TILE:(j + 1) * FF_TILE] = (_silu(g) * u).astype(BF16)
        yield
    ys = []
    for c in range(D_MODEL // FF_TILE):
        ys.append(jnp.dot(hid_ref[...], wdn_ref[:, c * FF_TILE:(c + 1) * FF_TILE], preferred_element_type=F32))
        yield
    finish(x + 0.5 * _rms(jnp.concatenate(ys, axis=1), gpost_ref[...]))


def _prep_steps(x_fn, w, outs, ncst_ref, cscr_ref, ns, t):
    gpre_ref, wmain_ref, wscal_ref, cw_ref, cb_ref, wrg_ref, brg_ref, lam_ref, alog_ref, dtb_ref = w
    a_ref, b_ref, rgate_ref, q_ref, k_ref, v_ref, zg_ref, scal_ref = outs
    r = ns * t
    cw_n = FF_TILE
    h = _rms(x_fn(), gpre_ref[...]).astype(BF16)
    yield

    def proj_cols(c0):
        return jnp.dot(h, wmain_ref[:, c0:c0 + cw_n], preferred_element_type=F32)

    def conv_cols(c0):
        cols = slice(c0, c0 + cw_n)
        x = proj_cols(c0).reshape(ns, t, cw_n)
        hist = cscr_ref[:, :, cols]
        taps = [cw_ref[j:j + 1, cols] for j in range(CONV_WIDTH)]
        u, hu = taps[0] * x, taps[0] * hist
        for j in range(1, CONV_WIDTH):
            sh = pltpu.roll(jnp.concatenate([hu, u], axis=1), 1, axis=1)
            u, hu = taps[j] * x + sh[:, CONV_PAD:], taps[j] * hist + sh[:, :CONV_PAD]
        new_hist = x[:, t - (CONV_WIDTH - 1):, :]
        ncst_ref[:, :, cols] = new_hist
        cscr_ref[:, CONV_PAD - (CONV_WIDTH - 1):, cols] = new_hist
        return u.reshape(r, cw_n)

    xr_parts = []
    for c0 in range(0, RG_WIDTH, cw_n):
        xr_parts.append(conv_cols(c0) + cb_ref[:, c0:c0 + cw_n])
        yield
    xr = jnp.concatenate(xr_parts, axis=1)
    gax = jnp.dot(xr.astype(BF16), wrg_ref[...], preferred_element_type=F32) + brg_ref[...]
    rr = _sigmoid(gax[:, :RG_WIDTH])
    ii = _sigmoid(gax[:, RG_WIDTH:])
    log_a = -RG_C * rr * _softplus(-lam_ref[...])
    a = jnp.exp(log_a)
    a_ref[...] = a.reshape(ns, t, RG_WIDTH)
    one_minus_a2 = -jnp.tanh(log_a) * (a * a + 1.0)
    b_ref[...] = (jnp.sqrt(one_minus_a2) * (ii * xr)).reshape(ns, t, RG_WIDTH)
    yield

    for base, ref, scale in ((RG_WIDTH, q_ref, DN_HEAD_DIM ** -0.5), (RG_WIDTH + DN_WIDTH, k_ref, None)):
        for c0 in range(0, DN_WIDTH, cw_n):
            act = _silu(conv_cols(base + c0))
            for hh in range(cw_n // DN_HEAD_DIM):
                xh = act[:, hh * DN_HEAD_DIM:(hh + 1) * DN_HEAD_DIM]
                inv = lax.rsqrt(jnp.sum(xh * xh, axis=-1, keepdims=True) + EPS)
                xn = xh * (inv if scale is None else inv * scale)
                lo = c0 + hh * DN_HEAD_DIM
                ref[:, :, lo:lo + DN_HEAD_DIM] = xn.reshape(ns, t, DN_HEAD_DIM)
            yield
    for c0 in range(0, DN_WIDTH, cw_n):
        v_ref[:, :, c0:c0 + cw_n] = _silu(conv_cols(RG_WIDTH + 2 * DN_WIDTH + c0)).reshape(ns, t, cw_n)
        yield

    for c0 in range(0, RG_WIDTH, cw_n):
        rgate_ref[:, :, c0:c0 + cw_n] = jax.nn.gelu(proj_cols(CONV_CH + c0)).reshape(ns, t, cw_n)
        yield
    for c0 in range(0, DN_WIDTH, cw_n):
        zg_ref[:, :, c0:c0 + cw_n] = _silu(proj_cols(CONV_CH + RG_WIDTH + c0)).reshape(ns, t, cw_n)
        yield

    scal = jnp.dot(h, wscal_ref[...], preferred_element_type=F32)
    lane = lax.broadcasted_iota(jnp.int32, scal.shape, 1)
    beta = _sigmoid(scal)
    g = -jnp.exp(alog_ref[...]) * _softplus(scal + dtb_ref[...])
    scal_ref[...] = jnp.where(lane < DN_HEADS, beta, g).reshape(ns, t, LANES)


def _recur_steps(ins, nw_ref, hout_ref, sout_ref, mix_ref, ns, t):
    a_ref, b_ref, rgate_ref, q_ref, k_ref, v_ref, zg_ref, scal_ref = ins
    r = ns * t
    seg = min(CHUNK, t)
    nseg = CHUNK // seg
    nchunk = r // CHUNK

    rowmod = lax.broadcasted_iota(jnp.int32, (t // SUBLANES, SUBLANES, RG_WIDTH), 1)

    def rg_seq(si, carry):
        a = a_ref[si].reshape(t // SUBLANES, SUBLANES, RG_WIDTH)
        b = b_ref[si].reshape(t // SUBLANES, SUBLANES, RG_WIDTH)
        for s in (1, 2, 4):
            m = rowmod >= s
            a_sh = pltpu.roll(a, s, axis=1)
            b_sh = pltpu.roll(b, s, axis=1)
            b = jnp.where(m, a * b_sh + b, b)
            a = jnp.where(m, a * a_sh, a)
        hc = hout_ref[si]
        for gi in range(t // SUBLANES):
            rows = slice(gi * SUBLANES, (gi + 1) * SUBLANES)
            hb = a[gi] * hc + b[gi]
            mix_ref[si, rows, :RG_WIDTH] = (hb * rgate_ref[si, rows, :]).astype(mix_ref.dtype)
            hc = hb[SUBLANES - 1:SUBLANES, :]
        hout_ref[si] = hc
        return carry

    lax.fori_loop(0, ns, rg_seq, 0)
    yield

    ri = lax.broadcasted_iota(jnp.int32, (CHUNK, CHUNK), 0)
    ci = lax.broadcasted_iota(jnp.int32, (CHUNK, CHUNK), 1)
    seg_shift = seg.bit_length() - 1
    same = (ri >> seg_shift) == (ci >> seg_shift)
    incl = jnp.logical_and(same, ri >= ci)
    strict = jnp.logical_and(same, ri > ci)
    incl_f = incl.astype(F32)
    eye = (ri == ci).astype(F32)
    sel = (lax.broadcasted_iota(jnp.int32, (SUBLANES, LANES), 0)
           == lax.broadcasted_iota(jnp.int32, (SUBLANES, LANES), 1)).astype(F32)
    nw = nw_ref[...]

    def chunk_index(c):
        if t >= CHUNK:
            per_seq = t // CHUNK
            return c // per_seq, slice((c % per_seq) * CHUNK, (c % per_seq + 1) * CHUNK)
        per_chunk = CHUNK // t
        return slice(c * per_chunk, (c + 1) * per_chunk), slice(None)

    def load_rows(ref, c, lanes):
        i0, i1 = chunk_index(c)
        return ref[i0, i1, lanes].reshape(CHUNK, -1)

    def store_rows(ref, c, lanes, val):
        i0, i1 = chunk_index(c)
        ref[i0, i1, lanes] = val if t >= CHUNK else val.reshape(CHUNK // t, t, -1)

    def state_index(c, jj):
        return c // (t // CHUNK) if t >= CHUNK else c * nseg + jj

    chains = []
    for c in range(nchunk):
        sc = load_rows(scal_ref, c, slice(None))
        gall = _fdot(incl_f, sc)
        gt = _fdot_nt(sel, gall)
        for hh in range(DN_HEADS):
            sl = slice(hh * DN_HEAD_DIM, (hh + 1) * DN_HEAD_DIM)
            chains.append(dict(
                c=c, hh=hh, sl=sl, beta=sc[:, hh:hh + 1],
                gc=gall[:, DN_HEADS + hh:DN_HEADS + hh + 1],
                gr=gt[DN_HEADS + hh:DN_HEADS + hh + 1, :]))
    yield
    for ch in chains:
        qh = load_rows(q_ref, ch["c"], ch["sl"])
        kh = load_rows(k_ref, ch["c"], ch["sl"])
        dec = jnp.where(incl, jnp.exp(jnp.where(incl, ch["gc"] - ch["gr"], 0.0)), 0.0)
        kb = kh * ch["beta"]
        kq = _bdot_nt(jnp.concatenate([kb, qh], axis=0), kh)
        ch.update(qh=qh, kh=kh, kb=kb,
                  x=jnp.where(strict, kq[:CHUNK] * dec, 0.0),
                  att=kq[CHUNK:] * dec)
    for ch in chains:
        ch["p"] = eye - ch["x"]
        if seg > 2:
            ch["x"] = _bdot(ch["x"], ch["x"])
    yield
    span = 4
    while span < seg:
        for ch in chains:
            both = _bdot(jnp.concatenate([ch["x"], ch["p"]], axis=0), ch["x"])
            ch["x"], ch["p"] = both[:CHUNK], ch["p"] + both[CHUNK:]
        yield
        span *= 2
    if seg > 2:
        for ch in chains:
            ch["p"] = ch["p"] + _bdot(ch["p"], ch["x"])
        yield
    for ch in chains:
        vh = load_rows(v_ref, ch["c"], ch["sl"])
        eg = jnp.exp(ch["gc"])
        sol = _bdot(ch["p"], jnp.concatenate([vh * ch["beta"], ch["kb"] * eg], axis=1))
        ch.update(u=sol[:, :DN_HEAD_DIM], w=sol[:, DN_HEAD_DIM:], qe=ch["qh"] * eg)
    yield
    for ch in chains:
        ch["s_old"], ch["rr"] = [], []
        for jj in range(nseg):
            rs = slice(jj * seg, (jj + 1) * seg)
            s_old = sout_ref[state_index(ch["c"], jj), ch["hh"]]
            ch["s_old"].append(s_old)
            ch["rr"].append(_bdot(jnp.concatenate([ch["w"][rs], ch["qe"][rs]], axis=0), s_old))
    yield
    for ch in chains:
        vnews = []
        for jj in range(nseg):
            rs = slice(jj * seg, (jj + 1) * seg)
            vnew = ch["u"][rs] - ch["rr"][jj][:seg]
            gl = ch["gc"][(jj + 1) * seg - 1:(jj + 1) * seg, :]
            kdec = ch["kh"][rs] * jnp.exp(gl - ch["gc"][rs])
            if nseg == 1:
                both = _bdot(jnp.concatenate([ch["att"], kdec.T], axis=0), vnew)
                ch["ov"], upd = both[:CHUNK], both[CHUNK:]
            else:
                upd = _bdot(kdec.T, vnew)
            sout_ref[state_index(ch["c"], jj), ch["hh"]] = ch["s_old"][jj] * jnp.exp(gl) + upd
            vnews.append(vnew)
        if nseg > 1:
            ch["ov"] = _bdot(ch["att"], jnp.concatenate(vnews, axis=0))
    yield
    for ch in chains:
        oint = ch["rr"][0][seg:] if nseg == 1 else jnp.concatenate([x[seg:] for x in ch["rr"]], axis=0)
        o = oint + ch["ov"]
        o = _rms(o, nw) * load_rows(zg_ref, ch["c"], ch["sl"])
        store_rows(mix_ref, ch["c"], slice(RG_WIDTH + ch["hh"] * DN_HEAD_DIM,
                                           RG_WIDTH + (ch["hh"] + 1) * DN_HEAD_DIM), o.astype(mix_ref.dtype))


def _kernel_a(x_ref, gpre1_ref, wup_ref, wdn_ref, gpost1_ref,
              gprem_ref, wmain_ref, wscal_ref, cw_ref, cb_ref, wrg_ref, brg_ref, lam_ref, alog_ref, dtb_ref,
              cst_ref, _prev_conv, *rest, tl):
    x1_ref, prep_ref, ncst_ref, hid_ref, xs_ref, cscr_ref = rest
    outs = _prep_views(prep_ref)
    g = pl.program_id(0)
    hist_rows = slice(CONV_PAD - (CONV_WIDTH - 1), CONV_PAD)

    @pl.when(g == 0)
    def _():
        xs_ref[...] = jnp.zeros_like(xs_ref)
        cscr_ref[...] = jnp.zeros_like(cscr_ref)

    @pl.when((g - 1) % tl.nt == 0)
    def _():
        cscr_ref[:, hist_rows, :] = cst_ref[...]

    def finish(x1):
        x1_ref[...] = x1.reshape(tl.ns, tl.t, D_MODEL)
        xs_ref[g % 2] = x1

    w = (gprem_ref, wmain_ref, wscal_ref, cw_ref, cb_ref, wrg_ref, brg_ref, lam_ref, alog_ref, dtb_ref)
    _interleave(
        _prep_steps(lambda: xs_ref[(g + 1) % 2], w, outs, ncst_ref, cscr_ref, tl.ns, tl.t),
        _ffn_steps(lambda: x_ref[...].reshape(tl.r, D_MODEL), gpre1_ref, wup_ref, wdn_ref, gpost1_ref, hid_ref,
                   finish))


def _call_a(x, layer, conv_state, prev_conv, w, *, ns, t):
    bn, l, _ = x.shape
    tl = _Tiling(bn, l, ns, t)
    f1, pw = w["f1"], w["prep"]
    big = {"wup", "wdn", "wmain", "wrg"}
    names = ["gpre", "wmain", "wscal", "cw", "cb", "wrg", "brg", "lam", "alog", "dtb"]
    args = [x, *f1] + [pw[n] for n in names] + [conv_state]
    in_specs = ([tl.rows(D_MODEL, "a"), _layer_spec(f1[0].shape, layer), _layer_spec(f1[1].shape, layer, single=True),
                 _layer_spec(f1[2].shape, layer, single=True), _layer_spec(f1[3].shape, layer)]
                + [_layer_spec(pw[n].shape, layer, single=n in big) for n in names]
                + [tl.state(conv_state.shape, layer, "b")])
    in_specs[5 + names.index("wmain")] = pl.BlockSpec(
        (None, D_MODEL, MAIN_COLS), lambda g: (layer, 0, 0), pipeline_mode=pl.Buffered(1))
    aliases = {len(args): 2}
    args.append(prev_conv)
    in_specs.append(pl.BlockSpec(memory_space=pl.ANY))
    return pl.pallas_call(
        functools.partial(_kernel_a, tl=tl),
        grid=(tl.n + 1,),
        in_specs=in_specs,
        out_specs=[tl.rows(D_MODEL, "a"), tl.rows(PREP_COLS, "b"), tl.state(conv_state.shape, layer, "b")],
        out_shape=[jax.ShapeDtypeStruct((bn, l, D_MODEL), F32), jax.ShapeDtypeStruct((bn, l, PREP_COLS), F32),
                   jax.ShapeDtypeStruct(conv_state.shape, F32)],
        scratch_shapes=[pltpu.VMEM((tl.r, D_FF), BF16), pltpu.VMEM((2, tl.r, D_MODEL), F32),
                        pltpu.VMEM((ns, CONV_PAD, CONV_CH), F32)],
        input_output_aliases=aliases,
        compiler_params=_cparams(),
        name="ffn1_prep",
    )(*args)


def _kernel_b(prep_ref, h0_ref, s0_ref, nw_ref, x1_ref, wout_ref, gmix_ref, _prev_h, _prev_s,
              xmid_ref, hout_ref, sout_ref, mix_ref, *, tl):
    g = pl.program_id(0)

    @pl.when(g % tl.nt == 0)
    def _():
        hout_ref[...] = h0_ref[...]
        sout_ref[...] = s0_ref[...]

    _interleave(_recur_steps(_prep_views(prep_ref), nw_ref, hout_ref, sout_ref, mix_ref, tl.ns, tl.t))
    m = mix_ref[...].reshape(tl.r, D_MODEL).astype(BF16)
    y = jnp.dot(m, wout_ref[...], preferred_element_type=F32)
    xm = x1_ref[...].reshape(tl.r, D_MODEL) + _rms(y, gmix_ref[...])
    xmid_ref[...] = xm.reshape(tl.ns, tl.t, D_MODEL)


def _call_b(prep, x1, layer, h0, s0, prev_h, prev_s, w, *, ns, t):
    bn, l, _ = prep.shape
    tl = _Tiling(bn, l, ns, t)
    args = [prep, h0, s0, w["nw"], x1, w["wout"], w["gmix"]]
    in_specs = [tl.rows(PREP_COLS, "a"), tl.state(h0.shape, layer, "a"), tl.state(s0.shape, layer, "a"),
                _layer_spec(w["nw"].shape, layer), tl.rows(D_MODEL, "a"),
                _layer_spec(w["wout"].shape, layer, single=True), _layer_spec(w["gmix"].shape, layer)]
    aliases = {len(args): 1, len(args) + 1: 2}
    args += [prev_h, prev_s]
    in_specs += [pl.BlockSpec(memory_space=pl.ANY)] * 2
    return pl.pallas_call(
        functools.partial(_kernel_b, tl=tl),
        grid=(tl.n,),
        in_specs=in_specs,
        out_specs=[tl.rows(D_MODEL, "a"), tl.state(h0.shape, layer, "a"), tl.state(s0.shape, layer, "a")],
        out_shape=[jax.ShapeDtypeStruct((bn, l, D_MODEL), F32), jax.ShapeDtypeStruct(h0.shape, F32),
                   jax.ShapeDtypeStruct(s0.shape, F32)],
        scratch_shapes=[pltpu.VMEM((ns, t, D_MODEL), F32)],
        input_output_aliases=aliases,
        compiler_params=_cparams(),
        name="recur_oproj",
    )(*args)


def _kernel_c(xp_ref, xs_ref, gpre_ref, wup_ref, wdn_ref, gpost_ref, *rest, nbp, final):
    rest = list(rest)
    gfin_ref = rest.pop(0) if final else None
    yp_ref, ys_ref, hid_ref = rest
    is_prompt = pl.program_id(0) < nbp

    def finish(out):
        if final:
            out = _rms(out, gfin_ref[...])

        @pl.when(is_prompt)
        def _():
            yp_ref[...] = out

        @pl.when(jnp.logical_not(is_prompt))
        def _():
            ys_ref[...] = out

    _interleave(_ffn_steps(lambda: jnp.where(is_prompt, xp_ref[...], xs_ref[...]),
                           gpre_ref, wup_ref, wdn_ref, gpost_ref, hid_ref, finish))


def _call_c(x_p, x_s, layer, w, gfin):
    n_p, n_s = x_p.shape[0], x_s.shape[0]
    nbp, nbs = n_p // TM, n_s // TM
    f2 = w["f2"]
    p_spec = pl.BlockSpec((TM, D_MODEL), lambda i: (jnp.minimum(i, nbp - 1), 0))
    s_spec = pl.BlockSpec((TM, D_MODEL), lambda i: (jnp.maximum(i - nbp, 0), 0))
    args = [x_p, x_s, *f2]
    in_specs = [p_spec, s_spec, _layer_spec(f2[0].shape, layer), _layer_spec(f2[1].shape, layer, single=True),
                _layer_spec(f2[2].shape, layer, single=True), _layer_spec(f2[3].shape, layer)]
    if gfin is not None:
        args.append(gfin)
        in_specs.append(pl.BlockSpec(gfin.shape, lambda i: (0, 0)))
    return pl.pallas_call(
        functools.partial(_kernel_c, nbp=nbp, final=gfin is not None),
        grid=(nbp + nbs,),
        in_specs=in_specs,
        out_specs=[p_spec, s_spec],
        out_shape=[jax.ShapeDtypeStruct((n_p, D_MODEL), F32), jax.ShapeDtypeStruct((n_s, D_MODEL), F32)],
        scratch_shapes=[pltpu.VMEM((TM, D_FF), BF16)],
        compiler_params=_cparams(),
        name="ffn2",
    )(*args)


def _lane_pad(mat, offset):
    depth, n = mat.shape
    out = jnp.zeros((depth, 1, LANES), F32)
    return lax.dynamic_update_slice(out, mat.reshape(depth, 1, n).astype(F32), (0, 0, offset))


def _weights(ffn1_norm_pre, ffn1_w_up, ffn1_w_down, ffn1_norm_post, mix_norm_pre, w_in, conv_w, conv_b_rg,
             rg_w_a, rg_b_a, rg_w_x, rg_b_x, rg_lambda, dn_a_log, dn_dt_bias, dn_norm_w, w_out, mix_norm_post,
             ffn2_norm_pre, ffn2_w_up, ffn2_w_down, ffn2_norm_post):
    depth = w_in.shape[0]
    row = lambda v: v.reshape(depth, 1, -1)
    wscal = jnp.zeros((depth, D_MODEL, LANES), BF16).at[:, :, :2 * DN_HEADS].set(
        w_in[:, :, MAIN_COLS:].astype(BF16))
    eye = jnp.eye(RG_BLOCKS, dtype=F32)
    bd = lambda w: jnp.einsum('lhij,hg->lhigj', w, eye).reshape(depth, RG_WIDTH, RG_WIDTH)
    return dict(
        f1=(row(ffn1_norm_pre), ffn1_w_up.astype(BF16), ffn1_w_down.astype(BF16), row(ffn1_norm_post)),
        f2=(row(ffn2_norm_pre), ffn2_w_up.astype(BF16), ffn2_w_down.astype(BF16), row(ffn2_norm_post)),
        prep=dict(gpre=row(mix_norm_pre), wmain=w_in.astype(BF16), wscal=wscal, cw=conv_w,
                  cb=row(conv_b_rg), wrg=jnp.concatenate([bd(rg_w_a), bd(rg_w_x)], axis=2).astype(BF16),
                  brg=jnp.concatenate([row(rg_b_a), row(rg_b_x)], axis=2), lam=row(rg_lambda),
                  alog=_lane_pad(dn_a_log, DN_HEADS), dtb=_lane_pad(dn_dt_bias, DN_HEADS)),
        nw=row(dn_norm_w),
        wout=w_out.astype(BF16),
        gmix=row(mix_norm_post),
    )


def kernel(x_prompt, x_sample, state_conv, state_rglru, state_delta, ffn1_norm_pre, ffn1_w_up, ffn1_w_down, ffn1_norm_post, mix_norm_pre, w_in, conv_w, conv_b_rg, rg_w_a, rg_b_a, rg_w_x, rg_b_x, rg_lambda, dn_a_log, dn_dt_bias, dn_norm_w, w_out, mix_norm_post, ffn2_norm_pre, ffn2_w_up, ffn2_w_down, ffn2_norm_post, final_norm):
    depth = w_in.shape[0]
    w = _weights(ffn1_norm_pre, ffn1_w_up, ffn1_w_down, ffn1_norm_post, mix_norm_pre, w_in, conv_w, conv_b_rg,
                 rg_w_a, rg_b_a, rg_w_x, rg_b_x, rg_lambda, dn_a_log, dn_dt_bias, dn_norm_w, w_out,
                 mix_norm_post, ffn2_norm_pre, ffn2_w_up, ffn2_w_down, ffn2_norm_post)
    bp, lp, _ = x_prompt.shape
    bs, ls, _ = x_sample.shape
    dt = x_prompt.dtype
    pops = [
        dict(x=x_prompt, blk_a=(1, 8 * CHUNK), blk_b=(bp, CHUNK),
             conv=jnp.zeros((depth, bp, CONV_WIDTH - 1, CONV_CH), dt),
             rg=jnp.zeros((depth, bp, 1, RG_WIDTH), dt),
             dn=jnp.zeros((depth, bp, DN_HEADS, DN_HEAD_DIM, DN_HEAD_DIM), dt)),
        dict(x=x_sample, blk_a=(4 * CHUNK // ls, ls), blk_b=(2 * CHUNK // ls, ls),
             conv=state_conv, rg=state_rglru.reshape(depth, bs, 1, RG_WIDTH), dn=state_delta),
    ]
    for p in pops:
        p.update(new_conv=jnp.zeros_like(p["conv"]), new_rg=jnp.zeros_like(p["rg"]),
                 new_dn=jnp.zeros_like(p["dn"]))
    pp, ps = pops
    flat = lambda v: v.reshape(-1, D_MODEL)
    for li in range(depth):
        for p in pops:
            p["x1"], prep, p["new_conv"] = _call_a(p["x"], li, p["conv"], p["new_conv"], w,
                                                   ns=p["blk_a"][0], t=p["blk_a"][1])
            p["xm"], p["new_rg"], p["new_dn"] = _call_b(prep, p["x1"], li, p["rg"], p["dn"], p["new_rg"],
                                                        p["new_dn"], w, ns=p["blk_b"][0], t=p["blk_b"][1])
        gfin = final_norm.reshape(1, -1) if li == depth - 1 else None
        y_p, y_s = _call_c(flat(pp["xm"]), flat(ps["xm"]), li, w, gfin)
        pp["x"], ps["x"] = y_p.reshape(pp["x"].shape), y_s.reshape(ps["x"].shape)
    return (pp["x"], ps["x"], pp["new_conv"], pp["new_rg"].reshape(depth, bp, RG_WIDTH), pp["new_dn"],
            ps["new_conv"], ps["new_rg"].reshape(depth, bs, RG_WIDTH), ps["new_dn"])
```

```python
import functools

import jax
import jax.numpy as jnp
from jax import lax
from jax.experimental import pallas as pl
from jax.experimental.pallas import tpu as pltpu

F32 = jnp.float32
BF16 = jnp.bfloat16

D_MODEL = 1024
D_FF = 2816
RG_WIDTH = 512
RG_BLOCKS = 8
RG_BLOCK = RG_WIDTH // RG_BLOCKS
RG_C = 8.0
DN_HEADS = 4
DN_HEAD_DIM = 128
DN_WIDTH = DN_HEADS * DN_HEAD_DIM
CONV_WIDTH = 4
CONV_CH = RG_WIDTH + 3 * DN_WIDTH
GATE_COLS = RG_WIDTH + DN_WIDTH
MAIN_COLS = CONV_CH + GATE_COLS
EPS = 1e-6

LANES = 128
SUBLANES = 8
CHUNK = 64
CONV_PAD = SUBLANES
FF_TILE = 256
TM = 512
VMEM_LIMIT = 56 * 1024 * 1024
PREP_WIDTHS = (RG_WIDTH, RG_WIDTH, RG_WIDTH, DN_WIDTH, DN_WIDTH, DN_WIDTH, DN_WIDTH, LANES)
PREP_COLS = sum(PREP_WIDTHS)


def _prep_views(prep_ref):
    views, off = [], 0
    for width in PREP_WIDTHS:
        views.append(prep_ref.at[:, :, off:off + width])
        off += width
    return tuple(views)


def _rms(x, g):
    return x * lax.rsqrt(jnp.mean(x * x, axis=-1, keepdims=True) + EPS) * g


def _sigmoid(x):
    return 0.5 + 0.5 * jnp.tanh(0.5 * x)


def _silu(x):
    hx = 0.5 * x
    return hx + hx * jnp.tanh(hx)


def _softplus(x):
    return jnp.maximum(x, 0.0) + jnp.log1p(jnp.exp(-jnp.abs(x)))


def _bdot(a, b):
    return jnp.dot(a.astype(BF16), b.astype(BF16), preferred_element_type=F32)


def _bdot_nt(a, b):
    return lax.dot_general(a.astype(BF16), b.astype(BF16), (((1,), (1,)), ((), ())),
                           preferred_element_type=F32)


def _fdot(a, b):
    return jnp.dot(a, b, preferred_element_type=F32, precision=lax.Precision.HIGHEST)


def _fdot_nt(a, b):
    return lax.dot_general(a, b, (((1,), (1,)), ((), ())), preferred_element_type=F32,
                           precision=lax.Precision.HIGHEST)


def _layer_spec(shape, layer, *, single=False):
    nd = len(shape)
    if nd == 2:
        return pl.BlockSpec(tuple(shape), lambda *_: (0, 0))
    kw = dict(pipeline_mode=pl.Buffered(1)) if single else {}
    return pl.BlockSpec((None,) + tuple(shape[1:]), lambda *_: (layer,) + (0,) * (nd - 1), **kw)


def _row(ref, layer):
    return ref.at[layer:layer + 1]


class _Tiling:
    def __init__(self, bn, l, ns, t):
        assert bn % ns == 0 and l % t == 0
        self.ns, self.t, self.r = ns, t, ns * t
        self.nt = l // t
        self.n = (bn // ns) * self.nt

    def tile_a(self, g):
        return jnp.minimum(g, self.n - 1)

    def tile_b(self, g):
        return jnp.maximum(g - 1, 0)

    def rows(self, width, stage):
        tile = self.tile_a if stage == "a" else self.tile_b
        return pl.BlockSpec((self.ns, self.t, width),
                            lambda g: (tile(g) // self.nt, tile(g) % self.nt, 0))

    def state(self, shape, layer, stage):
        tile = self.tile_a if stage == "a" else self.tile_b
        nd = len(shape)
        return pl.BlockSpec((None, self.ns) + tuple(shape[2:]),
                            lambda g: (layer, tile(g) // self.nt) + (0,) * (nd - 2))


def _cparams():
    return pltpu.CompilerParams(dimension_semantics=("arbitrary",), vmem_limit_bytes=VMEM_LIMIT)


def _interleave(*stages):
    live = list(stages)
    while live:
        for st in list(live):
            try:
                next(st)
            except StopIteration:
                live.remove(st)


def _ffn_steps(x_fn, gpre_ref, wup_ref, wdn_ref, gpost_ref, hid_ref, finish):
    x = x_fn()
    h = _rms(x, gpre_ref[...]).astype(BF16)
    yield
    for j in range(D_FF // FF_TILE):
        hg = jnp.dot(h, wup_ref[:, j * FF_TILE:(j + 1) * FF_TILE], preferred_element_type=F32)
        u = jnp.dot(h, wup_ref[:, D_FF + j * FF_TILE:D_FF + (j + 1) * FF_TILE], preferred_element_type=F32)
        hid_ref[:, j * FF_TILE:(j + 1) * FF_TILE] = ((hg + hg * jnp.tanh(hg)) * u).astype(BF16)
        yield
    ys = []
    for c in range(D_MODEL // FF_TILE):
        ys.append(jnp.dot(hid_ref[...], wdn_ref[:, c * FF_TILE:(c + 1) * FF_TILE], preferred_element_type=F32))
        yield
    finish(x + 0.5 * _rms(jnp.concatenate(ys, axis=1), gpost_ref[...]))


def _prep_steps(x_fn, w, outs, ncst_ref, cscr_ref, ns, t):
    gpre_ref, wmain_ref, wscal_ref, cw_ref, cb_ref, wrg_ref, brg_ref, lam_ref, alog_ref, dtb_ref = w
    a_ref, b_ref, rgate_ref, q_ref, k_ref, v_ref, zg_ref, scal_ref = outs
    r = ns * t
    cw_n = FF_TILE
    h = _rms(x_fn(), gpre_ref[...]).astype(BF16)
    yield

    def proj_cols(c0):
        return jnp.dot(h, wmain_ref[:, c0:c0 + cw_n], preferred_element_type=F32)

    def conv_cols(c0):
        cols = slice(c0, c0 + cw_n)
        x = proj_cols(c0).reshape(ns, t, cw_n)
        hist = cscr_ref[:, :, cols]
        taps = [cw_ref[j:j + 1, cols] for j in range(CONV_WIDTH)]
        u, hu = taps[0] * x, taps[0] * hist
        for j in range(1, CONV_WIDTH):
            sh = pltpu.roll(jnp.concatenate([hu, u], axis=1), 1, axis=1)
            u, hu = taps[j] * x + sh[:, CONV_PAD:], taps[j] * hist + sh[:, :CONV_PAD]
        new_hist = x[:, t - (CONV_WIDTH - 1):, :]
        ncst_ref[:, :, cols] = new_hist
        cscr_ref[:, CONV_PAD - (CONV_WIDTH - 1):, cols] = new_hist
        return u.reshape(r, cw_n)

    xr_parts = []
    for c0 in range(0, RG_WIDTH, cw_n):
        xr_parts.append(conv_cols(c0) + cb_ref[:, c0:c0 + cw_n])
        yield
    xr = jnp.concatenate(xr_parts, axis=1)
    gax = jnp.dot(xr.astype(BF16), wrg_ref[...], preferred_element_type=F32) + brg_ref[...]
    rr = _sigmoid(gax[:, :RG_WIDTH])
    ii = _sigmoid(gax[:, RG_WIDTH:])
    log_a = -RG_C * rr * _softplus(-lam_ref[...])
    a = jnp.exp(log_a)
    a_ref[...] = a.reshape(ns, t, RG_WIDTH)
    one_minus_a2 = -jnp.tanh(log_a) * (a * a + 1.0)
    b_ref[...] = (jnp.sqrt(one_minus_a2) * (ii * xr)).reshape(ns, t, RG_WIDTH)
    yield

    for base, ref, scale in ((RG_WIDTH, q_ref, DN_HEAD_DIM ** -0.5), (RG_WIDTH + DN_WIDTH, k_ref, None)):
        for c0 in range(0, DN_WIDTH, cw_n):
            act = _silu(conv_cols(base + c0))
            for hh in range(cw_n // DN_HEAD_DIM):
                xh = act[:, hh * DN_HEAD_DIM:(hh + 1) * DN_HEAD_DIM]
                inv = lax.rsqrt(jnp.sum(xh * xh, axis=-1, keepdims=True) + EPS)
                xn = xh * (inv if scale is None else inv * scale)
                lo = c0 + hh * DN_HEAD_DIM
                ref[:, :, lo:lo + DN_HEAD_DIM] = xn.reshape(ns, t, DN_HEAD_DIM)
            yield
    for c0 in range(0, DN_WIDTH, cw_n):
        v_ref[:, :, c0:c0 + cw_n] = _silu(conv_cols(RG_WIDTH + 2 * DN_WIDTH + c0)).reshape(ns, t, cw_n)
        yield

    for c0 in range(0, RG_WIDTH, cw_n):
        rgate_ref[:, :, c0:c0 + cw_n] = jax.nn.gelu(proj_cols(CONV_CH + c0)).reshape(ns, t, cw_n)
        yield
    for c0 in range(0, DN_WIDTH, cw_n):
        zg_ref[:, :, c0:c0 + cw_n] = _silu(proj_cols(CONV_CH + RG_WIDTH + c0)).reshape(ns, t, cw_n)
        yield

    scal = jnp.dot(h, wscal_ref[...], preferred_element_type=F32)
    lane = lax.broadcasted_iota(jnp.int32, scal.shape, 1)
    beta = _sigmoid(scal)
    g = -jnp.exp(alog_ref[...]) * _softplus(scal + dtb_ref[...])
    scal_ref[...] = jnp.where(lane < DN_HEADS, beta, g).reshape(ns, t, LANES)


def _recur_steps(ins, nw_ref, hout_ref, sout_ref, mix_ref, ns, t):
    a_ref, b_ref, rgate_ref, q_ref, k_ref, v_ref, zg_ref, scal_ref = ins
    r = ns * t
    seg = min(CHUNK, t)
    nseg = CHUNK // seg
    nchunk = r // CHUNK

    rowmod = lax.broadcasted_iota(jnp.int32, (t // SUBLANES, SUBLANES, RG_WIDTH), 1)

    def rg_seq(si, carry):
        a = a_ref[si].reshape(t // SUBLANES, SUBLANES, RG_WIDTH)
        b = b_ref[si].reshape(t // SUBLANES, SUBLANES, RG_WIDTH)
        for s in (1, 2, 4):
            m = rowmod >= s
            a_sh = pltpu.roll(a, s, axis=1)
            b_sh = pltpu.roll(b, s, axis=1)
            b = jnp.where(m, a * b_sh + b, b)
            a = jnp.where(m, a * a_sh, a)
        hc = hout_ref[si]
        for gi in range(t // SUBLANES):
            rows = slice(gi * SUBLANES, (gi + 1) * SUBLANES)
            hb = a[gi] * hc + b[gi]
            mix_ref[si, rows, :RG_WIDTH] = (hb * rgate_ref[si, rows, :]).astype(mix_ref.dtype)
            hc = hb[SUBLANES - 1:SUBLANES, :]
        hout_ref[si] = hc
        return carry

    lax.fori_loop(0, ns, rg_seq, 0)
    yield

    ri = lax.broadcasted_iota(jnp.int32, (CHUNK, CHUNK), 0)
    ci = lax.broadcasted_iota(jnp.int32, (CHUNK, CHUNK), 1)
    seg_shift = seg.bit_length() - 1
    same = (ri >> seg_shift) == (ci >> seg_shift)
    incl = jnp.logical_and(same, ri >= ci)
    strict = jnp.logical_and(same, ri > ci)
    incl_f = incl.astype(F32)
    eye = (ri == ci).astype(F32)
    sel = (lax.broadcasted_iota(jnp.int32, (SUBLANES, LANES), 0)
           == lax.broadcasted_iota(jnp.int32, (SUBLANES, LANES), 1)).astype(F32)
    nw = nw_ref[...]

    def chunk_index(c):
        if t >= CHUNK:
            per_seq = t // CHUNK
            return c // per_seq, slice((c % per_seq) * CHUNK, (c % per_seq + 1) * CHUNK)
        per_chunk = CHUNK // t
        return slice(c * per_chunk, (c + 1) * per_chunk), slice(None)

    def load_rows(ref, c, lanes):
        i0, i1 = chunk_index(c)
        return ref[i0, i1, lanes].reshape(CHUNK, -1)

    def store_rows(ref, c, lanes, val):
        i0, i1 = chunk_index(c)
        ref[i0, i1, lanes] = val if t >= CHUNK else val.reshape(CHUNK // t, t, -1)

    def state_index(c, jj):
        return c // (t // CHUNK) if t >= CHUNK else c * nseg + jj

    chains = []
    for c in range(nchunk):
        sc = load_rows(scal_ref, c, slice(None))
        gall = _fdot(incl_f, sc)
        gt = _fdot_nt(sel, gall)
        for hh in range(DN_HEADS):
            sl = slice(hh * DN_HEAD_DIM, (hh + 1) * DN_HEAD_DIM)
            chains.append(dict(
                c=c, hh=hh, sl=sl, beta=sc[:, hh:hh + 1],
                gc=gall[:, DN_HEADS + hh:DN_HEADS + hh + 1],
                gr=gt[DN_HEADS + hh:DN_HEADS + hh + 1, :]))
    yield
    for ch in chains:
        qh = load_rows(q_ref, ch["c"], ch["sl"])
        kh = load_rows(k_ref, ch["c"], ch["sl"])
        dec = jnp.where(incl, jnp.exp(jnp.where(incl, ch["gc"] - ch["gr"], 0.0)), 0.0)
        kb = kh * ch["beta"]
        kq = _bdot_nt(jnp.concatenate([kb, qh], axis=0), kh)
        ch.update(qh=qh, kh=kh, kb=kb,
                  x=jnp.where(strict, kq[:CHUNK] * dec, 0.0),
                  att=kq[CHUNK:] * dec)
    for ch in chains:
        ch["p"] = eye - ch["x"]
        if seg > 2:
            ch["x"] = _bdot(ch["x"], ch["x"])
    yield
    span = 4
    while span < seg:
        for ch in chains:
            both = _bdot(jnp.concatenate([ch["x"], ch["p"]], axis=0), ch["x"])
            ch["x"], ch["p"] = both[:CHUNK], ch["p"] + both[CHUNK:]
        yield
        span *= 2
    if seg > 2:
        for ch in chains:
            ch["p"] = ch["p"] + _bdot(ch["p"], ch["x"])
        yield
    for ch in chains:
        vh = load_rows(v_ref, ch["c"], ch["sl"])
        eg = jnp.exp(ch["gc"])
        sol = _bdot(ch["p"], jnp.concatenate([vh * ch["beta"], ch["kb"] * eg], axis=1))
        ch.update(u=sol[:, :DN_HEAD_DIM], w=sol[:, DN_HEAD_DIM:], qe=ch["qh"] * eg)
    yield
    for ch in chains:
        ch["s_old"], ch["rr"] = [], []
        for jj in range(nseg):
            rs = slice(jj * seg, (jj + 1) * seg)
            s_old = sout_ref[state_index(ch["c"], jj), ch["hh"]]
            ch["s_old"].append(s_old)
            ch["rr"].append(_bdot(jnp.concatenate([ch["w"][rs], ch["qe"][rs]], axis=0), s_old))
    yield
    for ch in chains:
        vnews = []
        for jj in range(nseg):
            rs = slice(jj * seg, (jj + 1) * seg)
            vnew = ch["u"][rs] - ch["rr"][jj][:seg]
            gl = ch["gc"][(jj + 1) * seg - 1:(jj + 1) * seg, :]
            kdec = ch["kh"][rs] * jnp.exp(gl - ch["gc"][rs])
            if nseg == 1:
                both = _bdot(jnp.concatenate([ch["att"], kdec.T], axis=0), vnew)
                ch["ov"], upd = both[:CHUNK], both[CHUNK:]
            else:
                upd = _bdot(kdec.T, vnew)
            sout_ref[state_index(ch["c"], jj), ch["hh"]] = ch["s_old"][jj] * jnp.exp(gl) + upd
            vnews.append(vnew)
        if nseg > 1:
            ch["ov"] = _bdot(ch["att"], jnp.concatenate(vnews, axis=0))
    yield
    for ch in chains:
        oint = ch["rr"][0][seg:] if nseg == 1 else jnp.concatenate([x[seg:] for x in ch["rr"]], axis=0)
        o = oint + ch["ov"]
        o = _rms(o, nw) * load_rows(zg_ref, ch["c"], ch["sl"])
        store_rows(mix_ref, ch["c"], slice(RG_WIDTH + ch["hh"] * DN_HEAD_DIM,
                                           RG_WIDTH + (ch["hh"] + 1) * DN_HEAD_DIM), o.astype(mix_ref.dtype))


def _kernel_a(x_ref, gpre1_ref, wup_ref, wdn_ref, gpost1_ref,
              gprem_ref, wmain_ref, wscal_ref, cw_ref, cb_ref, wrg_ref, brg_ref, lam_ref, alog_ref, dtb_ref,
              cst_ref, _prev_conv, *rest, tl, layer):
    x1_ref, prep_ref, ncst_ref, hid_ref, xs_ref, cscr_ref = rest
    gpre1_ref, gpost1_ref, gprem_ref, cb_ref, brg_ref, lam_ref, alog_ref, dtb_ref = (
        _row(v, layer) for v in (gpre1_ref, gpost1_ref, gprem_ref, cb_ref, brg_ref, lam_ref, alog_ref, dtb_ref))
    outs = _prep_views(prep_ref)
    g = pl.program_id(0)
    hist_rows = slice(CONV_PAD - (CONV_WIDTH - 1), CONV_PAD)

    @pl.when(g == 0)
    def _():
        xs_ref[...] = jnp.zeros_like(xs_ref)
        cscr_ref[...] = jnp.zeros_like(cscr_ref)

    @pl.when((g - 1) % tl.nt == 0)
    def _():
        cscr_ref[:, hist_rows, :] = cst_ref[...]

    def finish(x1):
        x1_ref[...] = x1.reshape(tl.ns, tl.t, D_MODEL)
        xs_ref[g % 2] = x1

    w = (gprem_ref, wmain_ref, wscal_ref, cw_ref, cb_ref, wrg_ref, brg_ref, lam_ref, alog_ref, dtb_ref)
    _interleave(
        _prep_steps(lambda: xs_ref[(g + 1) % 2], w, outs, ncst_ref, cscr_ref, tl.ns, tl.t),
        _ffn_steps(lambda: x_ref[...].reshape(tl.r, D_MODEL), gpre1_ref, wup_ref, wdn_ref, gpost1_ref, hid_ref,
                   finish))


def _call_a(x, layer, conv_state, prev_conv, w, *, ns, t):
    bn, l, _ = x.shape
    tl = _Tiling(bn, l, ns, t)
    f1, pw = w["f1"], w["prep"]
    big = {"wup", "wdn", "wmain", "wrg"}
    names = ["gpre", "wmain", "wscal", "cw", "cb", "wrg", "brg", "lam", "alog", "dtb"]
    args = [x, *f1] + [pw[n] for n in names] + [conv_state]
    in_specs = ([tl.rows(D_MODEL, "a"), _layer_spec(f1[0].shape, layer), _layer_spec(f1[1].shape, layer, single=True),
                 _layer_spec(f1[2].shape, layer, single=True), _layer_spec(f1[3].shape, layer)]
                + [_layer_spec(pw[n].shape, layer, single=n in big) for n in names]
                + [tl.state(conv_state.shape, layer, "b")])
    in_specs[5 + names.index("wmain")] = pl.BlockSpec(
        (None, D_MODEL, MAIN_COLS), lambda g: (layer, 0, 0), pipeline_mode=pl.Buffered(1))
    aliases = {len(args): 2}
    args.append(prev_conv)
    in_specs.append(pl.BlockSpec(memory_space=pl.ANY))
    return pl.pallas_call(
        functools.partial(_kernel_a, tl=tl, layer=layer),
        grid=(tl.n + 1,),
        in_specs=in_specs,
        out_specs=[tl.rows(D_MODEL, "a"), tl.rows(PREP_COLS, "b"), tl.state(conv_state.shape, layer, "b")],
        out_shape=[jax.ShapeDtypeStruct((bn, l, D_MODEL), F32), jax.ShapeDtypeStruct((bn, l, PREP_COLS), F32),
                   jax.ShapeDtypeStruct(conv_state.shape, F32)],
        scratch_shapes=[pltpu.VMEM((tl.r, D_FF), BF16), pltpu.VMEM((2, tl.r, D_MODEL), F32),
                        pltpu.VMEM((ns, CONV_PAD, CONV_CH), F32)],
        input_output_aliases=aliases,
        compiler_params=_cparams(),
        name="ffn1_prep",
    )(*args)


def _kernel_b(prep_ref, h0_ref, s0_ref, nw_ref, x1_ref, wout_ref, gmix_ref, _prev_h, _prev_s,
              xmid_ref, hout_ref, sout_ref, mix_ref, *, tl, layer):
    nw_ref, gmix_ref = _row(nw_ref, layer), _row(gmix_ref, layer)
    g = pl.program_id(0)

    @pl.when(g % tl.nt == 0)
    def _():
        hout_ref[...] = h0_ref[...]
        sout_ref[...] = s0_ref[...]

    _interleave(_recur_steps(_prep_views(prep_ref), nw_ref, hout_ref, sout_ref, mix_ref, tl.ns, tl.t))
    m = mix_ref[...].reshape(tl.r, D_MODEL).astype(BF16)
    y = jnp.dot(m, wout_ref[...], preferred_element_type=F32)
    xm = x1_ref[...].reshape(tl.r, D_MODEL) + _rms(y, gmix_ref[...])
    xmid_ref[...] = xm.reshape(tl.ns, tl.t, D_MODEL)


def _call_b(prep, x1, layer, h0, s0, prev_h, prev_s, w, *, ns, t):
    bn, l, _ = prep.shape
    tl = _Tiling(bn, l, ns, t)
    args = [prep, h0, s0, w["nw"], x1, w["wout"], w["gmix"]]
    in_specs = [tl.rows(PREP_COLS, "a"), tl.state(h0.shape, layer, "a"), tl.state(s0.shape, layer, "a"),
                _layer_spec(w["nw"].shape, layer), tl.rows(D_MODEL, "a"),
                _layer_spec(w["wout"].shape, layer, single=True), _layer_spec(w["gmix"].shape, layer)]
    aliases = {len(args): 1, len(args) + 1: 2}
    args += [prev_h, prev_s]
    in_specs += [pl.BlockSpec(memory_space=pl.ANY)] * 2
    return pl.pallas_call(
        functools.partial(_kernel_b, tl=tl, layer=layer),
        grid=(tl.n,),
        in_specs=in_specs,
        out_specs=[tl.rows(D_MODEL, "a"), tl.state(h0.shape, layer, "a"), tl.state(s0.shape, layer, "a")],
        out_shape=[jax.ShapeDtypeStruct((bn, l, D_MODEL), F32), jax.ShapeDtypeStruct(h0.shape, F32),
                   jax.ShapeDtypeStruct(s0.shape, F32)],
        scratch_shapes=[pltpu.VMEM((ns, t, D_MODEL), F32)],
        input_output_aliases=aliases,
        compiler_params=_cparams(),
        name="recur_oproj",
    )(*args)


def _kernel_c(xp_ref, xs_ref, gpre_ref, wup_ref, wdn_ref, gpost_ref, *rest, nbp, final, layer):
    gpre_ref, gpost_ref = _row(gpre_ref, layer), _row(gpost_ref, layer)
    rest = list(rest)
    gfin_ref = rest.pop(0) if final else None
    yp_ref, ys_ref, hid_ref = rest
    is_prompt = pl.program_id(0) < nbp

    def finish(out):
        if final:
            out = _rms(out, gfin_ref[...])

        @pl.when(is_prompt)
        def _():
            yp_ref[...] = out

        @pl.when(jnp.logical_not(is_prompt))
        def _():
            ys_ref[...] = out

    _interleave(_ffn_steps(lambda: jnp.where(is_prompt, xp_ref[...], xs_ref[...]),
                           gpre_ref, wup_ref, wdn_ref, gpost_ref, hid_ref, finish))


def _call_c(x_p, x_s, layer, w, gfin):
    n_p, n_s = x_p.shape[0], x_s.shape[0]
    nbp, nbs = n_p // TM, n_s // TM
    f2 = w["f2"]
    p_spec = pl.BlockSpec((TM, D_MODEL), lambda i: (jnp.minimum(i, nbp - 1), 0))
    s_spec = pl.BlockSpec((TM, D_MODEL), lambda i: (jnp.maximum(i - nbp, 0), 0))
    args = [x_p, x_s, *f2]
    in_specs = [p_spec, s_spec, _layer_spec(f2[0].shape, layer), _layer_spec(f2[1].shape, layer, single=True),
                _layer_spec(f2[2].shape, layer, single=True), _layer_spec(f2[3].shape, layer)]
    if gfin is not None:
        args.append(gfin)
        in_specs.append(pl.BlockSpec(gfin.shape, lambda i: (0, 0)))
    return pl.pallas_call(
        functools.partial(_kernel_c, nbp=nbp, final=gfin is not None, layer=layer),
        grid=(nbp + nbs,),
        in_specs=in_specs,
        out_specs=[p_spec, s_spec],
        out_shape=[jax.ShapeDtypeStruct((n_p, D_MODEL), F32), jax.ShapeDtypeStruct((n_s, D_MODEL), F32)],
        scratch_shapes=[pltpu.VMEM((TM, D_FF), BF16)],
        compiler_params=_cparams(),
        name="ffn2",
    )(*args)


def _lane_pad(mat, offset):
    out = jnp.zeros((mat.shape[0], LANES), F32)
    return lax.dynamic_update_slice(out, mat.astype(F32), (0, offset))


def _weights(ffn1_norm_pre, ffn1_w_up, ffn1_w_down, ffn1_norm_post, mix_norm_pre, w_in, conv_w, conv_b_rg,
             rg_w_a, rg_b_a, rg_w_x, rg_b_x, rg_lambda, dn_a_log, dn_dt_bias, dn_norm_w, w_out, mix_norm_post,
             ffn2_norm_pre, ffn2_w_up, ffn2_w_down, ffn2_norm_post):
    depth = w_in.shape[0]
    row = lambda v: v
    half_gate = jnp.concatenate([jnp.full((D_FF,), 0.5, F32), jnp.ones((D_FF,), F32)])
    up = lambda v: (v * half_gate).astype(BF16)
    wscal = jnp.zeros((depth, D_MODEL, LANES), BF16).at[:, :, :2 * DN_HEADS].set(
        w_in[:, :, MAIN_COLS:].astype(BF16))
    eye = jnp.eye(RG_BLOCKS, dtype=F32)
    bd = lambda w: jnp.einsum('lhij,hg->lhigj', w, eye).reshape(depth, RG_WIDTH, RG_WIDTH)
    return dict(
        f1=(row(ffn1_norm_pre), up(ffn1_w_up), ffn1_w_down.astype(BF16), row(ffn1_norm_post)),
        f2=(row(ffn2_norm_pre), up(ffn2_w_up), ffn2_w_down.astype(BF16), row(ffn2_norm_post)),
        prep=dict(gpre=row(mix_norm_pre), wmain=w_in.astype(BF16), wscal=wscal, cw=conv_w,
                  cb=row(conv_b_rg), wrg=jnp.concatenate([bd(rg_w_a), bd(rg_w_x)], axis=2).astype(BF16),
                  brg=jnp.concatenate([rg_b_a, rg_b_x], axis=1), lam=row(rg_lambda),
                  alog=_lane_pad(dn_a_log, DN_HEADS), dtb=_lane_pad(dn_dt_bias, DN_HEADS)),
        nw=row(dn_norm_w),
        wout=w_out.astype(BF16),
        gmix=row(mix_norm_post),
    )


def kernel(x_prompt, x_sample, state_conv, state_rglru, state_delta, ffn1_norm_pre, ffn1_w_up, ffn1_w_down, ffn1_norm_post, mix_norm_pre, w_in, conv_w, conv_b_rg, rg_w_a, rg_b_a, rg_w_x, rg_b_x, rg_lambda, dn_a_log, dn_dt_bias, dn_norm_w, w_out, mix_norm_post, ffn2_norm_pre, ffn2_w_up, ffn2_w_down, ffn2_norm_post, final_norm):
    depth = w_in.shape[0]
    w = _weights(ffn1_norm_pre, ffn1_w_up, ffn1_w_down, ffn1_norm_post, mix_norm_pre, w_in, conv_w, conv_b_rg,
                 rg_w_a, rg_b_a, rg_w_x, rg_b_x, rg_lambda, dn_a_log, dn_dt_bias, dn_norm_w, w_out,
                 mix_norm_post, ffn2_norm_pre, ffn2_w_up, ffn2_w_down, ffn2_norm_post)
    bp, lp, _ = x_prompt.shape
    bs, ls, _ = x_sample.shape
    dt = x_prompt.dtype
    pops = [
        dict(x=x_prompt, blk_a=(1, 4 * CHUNK), blk_b=(bp, CHUNK),
             conv=jnp.zeros((depth, bp, CONV_WIDTH - 1, CONV_CH), dt),
             rg=jnp.zeros((depth, bp, 1, RG_WIDTH), dt),
             dn=jnp.zeros((depth, bp, DN_HEADS, DN_HEAD_DIM, DN_HEAD_DIM), dt)),
        dict(x=x_sample, blk_a=(4 * CHUNK // ls, ls), blk_b=(2 * CHUNK // ls, ls),
             conv=state_conv, rg=state_rglru.reshape(depth, bs, 1, RG_WIDTH), dn=state_delta),
    ]
    for p in pops:
        p.update(new_conv=jnp.zeros_like(p["conv"]), new_rg=jnp.zeros_like(p["rg"]),
                 new_dn=jnp.zeros_like(p["dn"]))
    pp, ps = pops
    flat = lambda v: v.reshape(-1, D_MODEL)
    for li in range(depth):
        for p in pops:
            p["x1"], prep, p["new_conv"] = _call_a(p["x"], li, p["conv"], p["new_conv"], w,
                                                   ns=p["blk_a"][0], t=p["blk_a"][1])
            p["xm"], p["new_rg"], p["new_dn"] = _call_b(prep, p["x1"], li, p["rg"], p["dn"], p["new_rg"],
                                                        p["new_dn"], w, ns=p["blk_b"][0], t=p["blk_b"][1])
        gfin = final_norm.reshape(1, -1) if li == depth - 1 else None
        y_p, y_s = _call_c(flat(pp["xm"]), flat(ps["xm"]), li, w, gfin)
        pp["x"], ps["x"] = y_p.reshape(pp["x"].shape), y_s.reshape(ps["x"].shape)
    return (pp["x"], ps["x"], pp["new_conv"], pp["new_rg"].reshape(depth, bp, RG_WIDTH), pp["new_dn"],
            ps["new_conv"], ps["new_rg"].reshape(depth, bs, RG_WIDTH), ps["new_dn"])
```

```python
import functools

import jax
import jax.numpy as jnp
from jax import lax
from jax.experimental import pallas as pl
from jax.experimental.pallas import tpu as pltpu

F32 = jnp.float32
BF16 = jnp.bfloat16

D_MODEL = 1024
D_FF = 2816
RG_WIDTH = 512
RG_BLOCKS = 8
RG_BLOCK = RG_WIDTH // RG_BLOCKS
RG_C = 8.0
DN_HEADS = 4
DN_HEAD_DIM = 128
DN_WIDTH = DN_HEADS * DN_HEAD_DIM
CONV_WIDTH = 4
CONV_CH = RG_WIDTH + 3 * DN_WIDTH
GATE_COLS = RG_WIDTH + DN_WIDTH
MAIN_COLS = CONV_CH + GATE_COLS
EPS = 1e-6

LANES = 128
SUBLANES = 8
CHUNK = 64
CONV_PAD = SUBLANES
FF_TILE = 256
TM = 512
VMEM_LIMIT = 56 * 1024 * 1024
PREP_WIDTHS = (RG_WIDTH, RG_WIDTH, RG_WIDTH, DN_WIDTH, DN_WIDTH, DN_WIDTH, DN_WIDTH, LANES)
PREP_COLS = sum(PREP_WIDTHS)


def _prep_views(prep_ref):
    views, off = [], 0
    for width in PREP_WIDTHS:
        views.append(prep_ref.at[:, :, off:off + width])
        off += width
    return tuple(views)


def _rms(x, g):
    return x * lax.rsqrt(jnp.mean(x * x, axis=-1, keepdims=True) + EPS) * g


def _sigmoid(x):
    return 0.5 + 0.5 * jnp.tanh(0.5 * x)


def _silu(x):
    hx = 0.5 * x
    return hx + hx * jnp.tanh(hx)


def _softplus(x):
    return jnp.maximum(x, 0.0) + jnp.log1p(jnp.exp(-jnp.abs(x)))


def _bdot(a, b):
    return jnp.dot(a.astype(BF16), b.astype(BF16), preferred_element_type=F32)


def _bdot_nt(a, b):
    return lax.dot_general(a.astype(BF16), b.astype(BF16), (((1,), (1,)), ((), ())),
                           preferred_element_type=F32)


def _fdot(a, b):
    return jnp.dot(a, b, preferred_element_type=F32, precision=lax.Precision.HIGHEST)


def _fdot_nt(a, b):
    return lax.dot_general(a, b, (((1,), (1,)), ((), ())), preferred_element_type=F32,
                           precision=lax.Precision.HIGHEST)


def _layer_spec(shape, layer, *, single=False):
    nd = len(shape)
    if nd == 2:
        return pl.BlockSpec(tuple(shape), lambda *_: (0, 0))
    kw = dict(pipeline_mode=pl.Buffered(1)) if single else {}
    return pl.BlockSpec((None,) + tuple(shape[1:]), lambda *_: (layer,) + (0,) * (nd - 1), **kw)


def _row(ref, layer):
    return ref.at[layer:layer + 1]


class _Tiling:
    def __init__(self, bn, l, ns, t):
        assert bn % ns == 0 and l % t == 0
        self.ns, self.t, self.r = ns, t, ns * t
        self.nt = l // t
        self.n = (bn // ns) * self.nt

    def tile_a(self, g):
        return jnp.minimum(g, self.n - 1)

    def tile_b(self, g):
        return jnp.maximum(g - 1, 0)

    def rows(self, width, stage):
        tile = self.tile_a if stage == "a" else self.tile_b
        return pl.BlockSpec((self.ns, self.t, width),
                            lambda g: (tile(g) // self.nt, tile(g) % self.nt, 0))

    def state(self, shape, layer, stage):
        tile = self.tile_a if stage == "a" else self.tile_b
        nd = len(shape)
        return pl.BlockSpec((None, self.ns) + tuple(shape[2:]),
                            lambda g: (layer, tile(g) // self.nt) + (0,) * (nd - 2))


def _cparams():
    return pltpu.CompilerParams(dimension_semantics=("arbitrary",), vmem_limit_bytes=VMEM_LIMIT)


def _interleave(*stages):
    live = list(stages)
    while live:
        for st in list(live):
            try:
                next(st)
            except StopIteration:
                live.remove(st)


def _ffn_steps(x_fn, gpre_ref, wup_ref, wdn_ref, gpost_ref, hid_ref, finish):
    x = x_fn()
    h = _rms(x, gpre_ref[...]).astype(BF16)
    yield
    for j in range(D_FF // FF_TILE):
        hg = jnp.dot(h, wup_ref[:, j * FF_TILE:(j + 1) * FF_TILE], preferred_element_type=F32)
        u = jnp.dot(h, wup_ref[:, D_FF + j * FF_TILE:D_FF + (j + 1) * FF_TILE], preferred_element_type=F32)
        hid_ref[:, j * FF_TILE:(j + 1) * FF_TILE] = ((hg + hg * jnp.tanh(hg)) * u).astype(BF16)
        yield
    ys = []
    for c in range(D_MODEL // FF_TILE):
        ys.append(jnp.dot(hid_ref[...], wdn_ref[:, c * FF_TILE:(c + 1) * FF_TILE], preferred_element_type=F32))
        yield
    finish(x + 0.5 * _rms(jnp.concatenate(ys, axis=1), gpost_ref[...]))


def _prep_steps(x_fn, w, outs, ncst_ref, cscr_ref, ns, t):
    gpre_ref, wmain_ref, wscal_ref, cw_ref, cb_ref, wrg_ref, brg_ref, lam_ref, alog_ref, dtb_ref = w
    a_ref, b_ref, rgate_ref, q_ref, k_ref, v_ref, zg_ref, scal_ref = outs
    r = ns * t
    cw_n = FF_TILE
    h = _rms(x_fn(), gpre_ref[...]).astype(BF16)
    yield

    def proj_cols(c0):
        return jnp.dot(h, wmain_ref[:, c0:c0 + cw_n], preferred_element_type=F32)

    def conv_cols(c0):
        cols = slice(c0, c0 + cw_n)
        x = proj_cols(c0).reshape(ns, t, cw_n)
        hist = cscr_ref[:, :, cols]
        taps = [cw_ref[j:j + 1, cols] for j in range(CONV_WIDTH)]
        u, hu = taps[0] * x, taps[0] * hist
        for j in range(1, CONV_WIDTH):
            sh = pltpu.roll(jnp.concatenate([hu, u], axis=1), 1, axis=1)
            u, hu = taps[j] * x + sh[:, CONV_PAD:], taps[j] * hist + sh[:, :CONV_PAD]
        new_hist = x[:, t - (CONV_WIDTH - 1):, :]
        ncst_ref[:, :, cols] = new_hist
        cscr_ref[:, CONV_PAD - (CONV_WIDTH - 1):, cols] = new_hist
        return u.reshape(r, cw_n)

    xr_parts = []
    for c0 in range(0, RG_WIDTH, cw_n):
        xr_parts.append(conv_cols(c0) + cb_ref[:, c0:c0 + cw_n])
        yield
    xr = jnp.concatenate(xr_parts, axis=1)
    gax = jnp.dot(xr.astype(BF16), wrg_ref[...], preferred_element_type=F32) + brg_ref[...]
    rr = _sigmoid(gax[:, :RG_WIDTH])
    ii = _sigmoid(gax[:, RG_WIDTH:])
    log_a = -RG_C * rr * _softplus(-lam_ref[...])
    a = jnp.exp(log_a)
    a_ref[...] = a.reshape(ns, t, RG_WIDTH)
    one_minus_a2 = -jnp.tanh(log_a) * (a * a + 1.0)
    b_ref[...] = (jnp.sqrt(one_minus_a2) * (ii * xr)).reshape(ns, t, RG_WIDTH)
    yield

    for base, ref, scale in ((RG_WIDTH, q_ref, DN_HEAD_DIM ** -0.5), (RG_WIDTH + DN_WIDTH, k_ref, None)):
        for c0 in range(0, DN_WIDTH, cw_n):
            act = _silu(conv_cols(base + c0))
            for hh in range(cw_n // DN_HEAD_DIM):
                xh = act[:, hh * DN_HEAD_DIM:(hh + 1) * DN_HEAD_DIM]
                inv = lax.rsqrt(jnp.sum(xh * xh, axis=-1, keepdims=True) + EPS)
                xn = xh * (inv if scale is None else inv * scale)
                lo = c0 + hh * DN_HEAD_DIM
                ref[:, :, lo:lo + DN_HEAD_DIM] = xn.reshape(ns, t, DN_HEAD_DIM)
            yield
    for c0 in range(0, DN_WIDTH, cw_n):
        v_ref[:, :, c0:c0 + cw_n] = _silu(conv_cols(RG_WIDTH + 2 * DN_WIDTH + c0)).reshape(ns, t, cw_n)
        yield

    for c0 in range(0, RG_WIDTH, cw_n):
        rgate_ref[:, :, c0:c0 + cw_n] = jax.nn.gelu(proj_cols(CONV_CH + c0)).reshape(ns, t, cw_n)
        yield
    for c0 in range(0, DN_WIDTH, cw_n):
        zg_ref[:, :, c0:c0 + cw_n] = _silu(proj_cols(CONV_CH + RG_WIDTH + c0)).reshape(ns, t, cw_n)
        yield

    scal = jnp.dot(h, wscal_ref[...], preferred_element_type=F32)
    lane = lax.broadcasted_iota(jnp.int32, scal.shape, 1)
    beta = _sigmoid(scal)
    g = -jnp.exp(alog_ref[...]) * _softplus(scal + dtb_ref[...])
    scal_ref[...] = jnp.where(lane < DN_HEADS, beta, g).reshape(ns, t, LANES)


def _recur_steps(ins, nw_ref, hout_ref, sout_ref, mix_ref, ns, t):
    a_ref, b_ref, rgate_ref, q_ref, k_ref, v_ref, zg_ref, scal_ref = ins
    r = ns * t
    seg = min(CHUNK, t)
    nseg = CHUNK // seg
    nchunk = r // CHUNK

    rowmod = lax.broadcasted_iota(jnp.int32, (t // SUBLANES, SUBLANES, RG_WIDTH), 1)

    def rg_seq(si, carry):
        a = a_ref[si].reshape(t // SUBLANES, SUBLANES, RG_WIDTH)
        b = b_ref[si].reshape(t // SUBLANES, SUBLANES, RG_WIDTH)
        for s in (1, 2, 4):
            m = rowmod >= s
            a_sh = pltpu.roll(a, s, axis=1)
            b_sh = pltpu.roll(b, s, axis=1)
            b = jnp.where(m, a * b_sh + b, b)
            a = jnp.where(m, a * a_sh, a)
        hc = hout_ref[si]
        for gi in range(t // SUBLANES):
            rows = slice(gi * SUBLANES, (gi + 1) * SUBLANES)
            hb = a[gi] * hc + b[gi]
            mix_ref[si, rows, :RG_WIDTH] = (hb * rgate_ref[si, rows, :]).astype(mix_ref.dtype)
            hc = hb[SUBLANES - 1:SUBLANES, :]
        hout_ref[si] = hc
        return carry

    lax.fori_loop(0, ns, rg_seq, 0)
    yield

    ri = lax.broadcasted_iota(jnp.int32, (CHUNK, CHUNK), 0)
    ci = lax.broadcasted_iota(jnp.int32, (CHUNK, CHUNK), 1)
    seg_shift = seg.bit_length() - 1
    same = (ri >> seg_shift) == (ci >> seg_shift)
    incl = jnp.logical_and(same, ri >= ci)
    strict = jnp.logical_and(same, ri > ci)
    incl_f = incl.astype(F32)
    eye = (ri == ci).astype(F32)
    sel = (lax.broadcasted_iota(jnp.int32, (SUBLANES, LANES), 0)
           == lax.broadcasted_iota(jnp.int32, (SUBLANES, LANES), 1)).astype(F32)
    nw = nw_ref[...]

    def chunk_index(c):
        if t >= CHUNK:
            per_seq = t // CHUNK
            return c // per_seq, slice((c % per_seq) * CHUNK, (c % per_seq + 1) * CHUNK)
        per_chunk = CHUNK // t
        return slice(c * per_chunk, (c + 1) * per_chunk), slice(None)

    def load_rows(ref, c, lanes):
        i0, i1 = chunk_index(c)
        return ref[i0, i1, lanes].reshape(CHUNK, -1)

    def store_rows(ref, c, lanes, val):
        i0, i1 = chunk_index(c)
        ref[i0, i1, lanes] = val if t >= CHUNK else val.reshape(CHUNK // t, t, -1)

    def state_index(c, jj):
        return c // (t // CHUNK) if t >= CHUNK else c * nseg + jj

    chains = []
    for c in range(nchunk):
        sc = load_rows(scal_ref, c, slice(None))
        gall = _fdot(incl_f, sc)
        gt = _fdot_nt(sel, gall)
        for hh in range(DN_HEADS):
            sl = slice(hh * DN_HEAD_DIM, (hh + 1) * DN_HEAD_DIM)
            chains.append(dict(
                c=c, hh=hh, sl=sl, beta=sc[:, hh:hh + 1],
                gc=gall[:, DN_HEADS + hh:DN_HEADS + hh + 1],
                gr=gt[DN_HEADS + hh:DN_HEADS + hh + 1, :]))
    yield
    for ch in chains:
        qh = load_rows(q_ref, ch["c"], ch["sl"])
        kh = load_rows(k_ref, ch["c"], ch["sl"])
        dec = jnp.where(incl, jnp.exp(jnp.where(incl, ch["gc"] - ch["gr"], 0.0)), 0.0)
        kb = kh * ch["beta"]
        kq = _bdot_nt(jnp.concatenate([kb, qh], axis=0), kh)
        ch.update(qh=qh, kh=kh, kb=kb,
                  x=jnp.where(strict, kq[:CHUNK] * dec, 0.0),
                  att=kq[CHUNK:] * dec)
    for ch in chains:
        ch["p"] = eye - ch["x"]
        if seg > 2:
            ch["x"] = _bdot(ch["x"], ch["x"])
    yield
    span = 4
    while span < seg:
        for ch in chains:
            both = _bdot(jnp.concatenate([ch["x"], ch["p"]], axis=0), ch["x"])
            ch["x"], ch["p"] = both[:CHUNK], ch["p"] + both[CHUNK:]
        yield
        span *= 2
    if seg > 2:
        for ch in chains:
            ch["p"] = ch["p"] + _bdot(ch["p"], ch["x"])
        yield
    for ch in chains:
        vh = load_rows(v_ref, ch["c"], ch["sl"])
        eg = jnp.exp(ch["gc"])
        sol = _bdot(ch["p"], jnp.concatenate([vh * ch["beta"], ch["kb"] * eg], axis=1))
        ch.update(u=sol[:, :DN_HEAD_DIM], w=sol[:, DN_HEAD_DIM:], qe=ch["qh"] * eg)
    yield
    for ch in chains:
        ch["s_old"], ch["rr"] = [], []
        for jj in range(nseg):
            rs = slice(jj * seg, (jj + 1) * seg)
            s_old = sout_ref[state_index(ch["c"], jj), ch["hh"]]
            ch["s_old"].append(s_old)
            ch["rr"].append(_bdot(jnp.concatenate([ch["w"][rs], ch["qe"][rs]], axis=0), s_old))
    yield
    for ch in chains:
        vnews = []
        for jj in range(nseg):
            rs = slice(jj * seg, (jj + 1) * seg)
            vnew = ch["u"][rs] - ch["rr"][jj][:seg]
            gl = ch["gc"][(jj + 1) * seg - 1:(jj + 1) * seg, :]
            kdec = ch["kh"][rs] * jnp.exp(gl - ch["gc"][rs])
            if nseg == 1:
                both = _bdot(jnp.concatenate([ch["att"], kdec.T], axis=0), vnew)
                ch["ov"], upd = both[:CHUNK], both[CHUNK:]
            else:
                upd = _bdot(kdec.T, vnew)
            sout_ref[state_index(ch["c"], jj), ch["hh"]] = ch["s_old"][jj] * jnp.exp(gl) + upd
            vnews.append(vnew)
        if nseg > 1:
            ch["ov"] = _bdot(ch["att"], jnp.concatenate(vnews, axis=0))
    yield
    for ch in chains:
        oint = ch["rr"][0][seg:] if nseg == 1 else jnp.concatenate([x[seg:] for x in ch["rr"]], axis=0)
        o = oint + ch["ov"]
        o = _rms(o, nw) * load_rows(zg_ref, ch["c"], ch["sl"])
        store_rows(mix_ref, ch["c"], slice(RG_WIDTH + ch["hh"] * DN_HEAD_DIM,
                                           RG_WIDTH + (ch["hh"] + 1) * DN_HEAD_DIM), o.astype(mix_ref.dtype))


def _kernel_a(x_ref, gpre1_ref, wup_ref, wdn_ref, gpost1_ref,
              gprem_ref, wmain_ref, wscal_ref, cw_ref, cb_ref, wrg_ref, brg_ref, lam_ref, alog_ref, dtb_ref,
              cst_ref, _prev_conv, *rest, tl, layer):
    x1_ref, prep_ref, ncst_ref, hid_ref, xs_ref, cscr_ref = rest
    gpre1_ref, gpost1_ref, gprem_ref, cb_ref, brg_ref, lam_ref, alog_ref, dtb_ref = (
        _row(v, layer) for v in (gpre1_ref, gpost1_ref, gprem_ref, cb_ref, brg_ref, lam_ref, alog_ref, dtb_ref))
    outs = _prep_views(prep_ref)
    g = pl.program_id(0)
    hist_rows = slice(CONV_PAD - (CONV_WIDTH - 1), CONV_PAD)

    @pl.when(g == 0)
    def _():
        xs_ref[...] = jnp.zeros_like(xs_ref)
        cscr_ref[...] = jnp.zeros_like(cscr_ref)

    @pl.when((g - 1) % tl.nt == 0)
    def _():
        cscr_ref[:, hist_rows, :] = cst_ref[...]

    def finish(x1):
        x1_ref[...] = x1.reshape(tl.ns, tl.t, D_MODEL)
        xs_ref[g % 2] = x1

    w = (gprem_ref, wmain_ref, wscal_ref, cw_ref, cb_ref, wrg_ref, brg_ref, lam_ref, alog_ref, dtb_ref)
    _interleave(
        _prep_steps(lambda: xs_ref[(g + 1) % 2], w, outs, ncst_ref, cscr_ref, tl.ns, tl.t),
        _ffn_steps(lambda: x_ref[...].reshape(tl.r, D_MODEL), gpre1_ref, wup_ref, wdn_ref, gpost1_ref, hid_ref,
                   finish))


def _call_a(x, layer, conv_state, prev_conv, w, *, ns, t):
    bn, l, _ = x.shape
    tl = _Tiling(bn, l, ns, t)
    f1, pw = w["f1"], w["prep"]
    big = {"wup", "wdn", "wmain", "wrg"}
    names = ["gpre", "wmain", "wscal", "cw", "cb", "wrg", "brg", "lam", "alog", "dtb"]
    args = [x, *f1] + [pw[n] for n in names] + [conv_state]
    in_specs = ([tl.rows(D_MODEL, "a"), _layer_spec(f1[0].shape, layer), _layer_spec(f1[1].shape, layer, single=True),
                 _layer_spec(f1[2].shape, layer, single=True), _layer_spec(f1[3].shape, layer)]
                + [_layer_spec(pw[n].shape, layer, single=n in big) for n in names]
                + [tl.state(conv_state.shape, layer, "b")])
    in_specs[5 + names.index("wmain")] = pl.BlockSpec(
        (None, D_MODEL, MAIN_COLS), lambda g: (layer, 0, 0), pipeline_mode=pl.Buffered(1))
    aliases = {len(args): 2}
    args.append(prev_conv)
    in_specs.append(pl.BlockSpec(memory_space=pl.ANY))
    return pl.pallas_call(
        functools.partial(_kernel_a, tl=tl, layer=layer),
        grid=(tl.n + 1,),
        in_specs=in_specs,
        out_specs=[tl.rows(D_MODEL, "a"), tl.rows(PREP_COLS, "b"), tl.state(conv_state.shape, layer, "b")],
        out_shape=[jax.ShapeDtypeStruct((bn, l, D_MODEL), F32), jax.ShapeDtypeStruct((bn, l, PREP_COLS), F32),
                   jax.ShapeDtypeStruct(conv_state.shape, F32)],
        scratch_shapes=[pltpu.VMEM((tl.r, D_FF), BF16), pltpu.VMEM((2, tl.r, D_MODEL), F32),
                        pltpu.VMEM((ns, CONV_PAD, CONV_CH), F32)],
        input_output_aliases=aliases,
        compiler_params=_cparams(),
        name="ffn1_prep",
    )(*args)


def _kernel_b(prep_ref, h0_ref, s0_ref, nw_ref, x1_ref, wout_ref, gmix_ref, _prev_h, _prev_s,
              xmid_ref, hout_ref, sout_ref, mix_ref, *, tl, layer):
    nw_ref, gmix_ref = _row(nw_ref, layer), _row(gmix_ref, layer)
    g = pl.program_id(0)

    @pl.when(g % tl.nt == 0)
    def _():
        hout_ref[...] = h0_ref[...]
        sout_ref[...] = s0_ref[...]

    _interleave(_recur_steps(_prep_views(prep_ref), nw_ref, hout_ref, sout_ref, mix_ref, tl.ns, tl.t))
    m = mix_ref[...].reshape(tl.r, D_MODEL).astype(BF16)
    y = jnp.dot(m, wout_ref[...], preferred_element_type=F32)
    xm = x1_ref[...].reshape(tl.r, D_MODEL) + _rms(y, gmix_ref[...])
    xmid_ref[...] = xm.reshape(tl.ns, tl.t, D_MODEL)


def _call_b(prep, x1, layer, h0, s0, prev_h, prev_s, w, *, ns, t):
    bn, l, _ = prep.shape
    tl = _Tiling(bn, l, ns, t)
    args = [prep, h0, s0, w["nw"], x1, w["wout"], w["gmix"]]
    in_specs = [tl.rows(PREP_COLS, "a"), tl.state(h0.shape, layer, "a"), tl.state(s0.shape, layer, "a"),
                _layer_spec(w["nw"].shape, layer), tl.rows(D_MODEL, "a"),
                _layer_spec(w["wout"].shape, layer, single=True), _layer_spec(w["gmix"].shape, layer)]
    aliases = {len(args): 1, len(args) + 1: 2}
    args += [prev_h, prev_s]
    in_specs += [pl.BlockSpec(memory_space=pl.ANY)] * 2
    return pl.pallas_call(
        functools.partial(_kernel_b, tl=tl, layer=layer),
        grid=(tl.n,),
        in_specs=in_specs,
        out_specs=[tl.rows(D_MODEL, "a"), tl.state(h0.shape, layer, "a"), tl.state(s0.shape, layer, "a")],
        out_shape=[jax.ShapeDtypeStruct((bn, l, D_MODEL), F32), jax.ShapeDtypeStruct(h0.shape, F32),
                   jax.ShapeDtypeStruct(s0.shape, F32)],
        scratch_shapes=[pltpu.VMEM((ns, t, D_MODEL), F32)],
        input_output_aliases=aliases,
        compiler_params=_cparams(),
        name="recur_oproj",
    )(*args)


def _kernel_c(xp_ref, xs_ref, gpre_ref, wup_ref, wdn_ref, gpost_ref, *rest, nbp, final, layer):
    gpre_ref, gpost_ref = _row(gpre_ref, layer), _row(gpost_ref, layer)
    rest = list(rest)
    gfin_ref = rest.pop(0) if final else None
    yp_ref, ys_ref, hid_ref = rest
    is_prompt = pl.program_id(0) < nbp

    def finish(out):
        if final:
            out = _rms(out, gfin_ref[...])

        @pl.when(is_prompt)
        def _():
            yp_ref[...] = out

        @pl.when(jnp.logical_not(is_prompt))
        def _():
            ys_ref[...] = out

    _interleave(_ffn_steps(lambda: jnp.where(is_prompt, xp_ref[...], xs_ref[...]),
                           gpre_ref, wup_ref, wdn_ref, gpost_ref, hid_ref, finish))


def _call_c(x_p, x_s, layer, w, gfin):
    n_p, n_s = x_p.shape[0], x_s.shape[0]
    nbp, nbs = n_p // TM, n_s // TM
    f2 = w["f2"]
    p_spec = pl.BlockSpec((TM, D_MODEL), lambda i: (jnp.minimum(i, nbp - 1), 0))
    s_spec = pl.BlockSpec((TM, D_MODEL), lambda i: (jnp.maximum(i - nbp, 0), 0))
    args = [x_p, x_s, *f2]
    in_specs = [p_spec, s_spec, _layer_spec(f2[0].shape, layer), _layer_spec(f2[1].shape, layer, single=True),
                _layer_spec(f2[2].shape, layer, single=True), _layer_spec(f2[3].shape, layer)]
    if gfin is not None:
        args.append(gfin)
        in_specs.append(pl.BlockSpec(gfin.shape, lambda i: (0, 0)))
    return pl.pallas_call(
        functools.partial(_kernel_c, nbp=nbp, final=gfin is not None, layer=layer),
        grid=(nbp + nbs,),
        in_specs=in_specs,
        out_specs=[p_spec, s_spec],
        out_shape=[jax.ShapeDtypeStruct((n_p, D_MODEL), F32), jax.ShapeDtypeStruct((n_s, D_MODEL), F32)],
        scratch_shapes=[pltpu.VMEM((TM, D_FF), BF16)],
        compiler_params=_cparams(),
        name="ffn2",
    )(*args)


def _lane_pad(mat, offset):
    out = jnp.zeros((mat.shape[0], LANES), F32)
    return lax.dynamic_update_slice(out, mat.astype(F32), (0, offset))


def _weights(ffn1_norm_pre, ffn1_w_up, ffn1_w_down, ffn1_norm_post, mix_norm_pre, w_in, conv_w, conv_b_rg,
             rg_w_a, rg_b_a, rg_w_x, rg_b_x, rg_lambda, dn_a_log, dn_dt_bias, dn_norm_w, w_out, mix_norm_post,
             ffn2_norm_pre, ffn2_w_up, ffn2_w_down, ffn2_norm_post):
    depth = w_in.shape[0]
    row = lambda v: v
    half_gate = jnp.concatenate([jnp.full((D_FF,), 0.5, F32), jnp.ones((D_FF,), F32)])
    up = lambda v: (v * half_gate).astype(BF16)
    wscal = jnp.zeros((depth, D_MODEL, LANES), BF16).at[:, :, :2 * DN_HEADS].set(
        w_in[:, :, MAIN_COLS:].astype(BF16))
    eye = jnp.eye(RG_BLOCKS, dtype=F32)
    bd = lambda w: jnp.einsum('lhij,hg->lhigj', w, eye).reshape(depth, RG_WIDTH, RG_WIDTH)
    return dict(
        f1=(row(ffn1_norm_pre), up(ffn1_w_up), ffn1_w_down.astype(BF16), row(ffn1_norm_post)),
        f2=(row(ffn2_norm_pre), up(ffn2_w_up), ffn2_w_down.astype(BF16), row(ffn2_norm_post)),
        prep=dict(gpre=row(mix_norm_pre), wmain=w_in.astype(BF16), wscal=wscal, cw=conv_w,
                  cb=row(conv_b_rg), wrg=jnp.concatenate([bd(rg_w_a), bd(rg_w_x)], axis=2).astype(BF16),
                  brg=jnp.concatenate([rg_b_a, rg_b_x], axis=1), lam=row(rg_lambda),
                  alog=_lane_pad(dn_a_log, DN_HEADS), dtb=_lane_pad(dn_dt_bias, DN_HEADS)),
        nw=row(dn_norm_w),
        wout=w_out.astype(BF16),
        gmix=row(mix_norm_post),
    )


def kernel(x_prompt, x_sample, state_conv, state_rglru, state_delta, ffn1_norm_pre, ffn1_w_up, ffn1_w_down, ffn1_norm_post, mix_norm_pre, w_in, conv_w, conv_b_rg, rg_w_a, rg_b_a, rg_w_x, rg_b_x, rg_lambda, dn_a_log, dn_dt_bias, dn_norm_w, w_out, mix_norm_post, ffn2_norm_pre, ffn2_w_up, ffn2_w_down, ffn2_norm_post, final_norm):
    depth = w_in.shape[0]
    w = _weights(ffn1_norm_pre, ffn1_w_up, ffn1_w_down, ffn1_norm_post, mix_norm_pre, w_in, conv_w, conv_b_rg,
                 rg_w_a, rg_b_a, rg_w_x, rg_b_x, rg_lambda, dn_a_log, dn_dt_bias, dn_norm_w, w_out,
                 mix_norm_post, ffn2_norm_pre, ffn2_w_up, ffn2_w_down, ffn2_norm_post)
    bp, lp, _ = x_prompt.shape
    bs, ls, _ = x_sample.shape
    dt = x_prompt.dtype
    pops = [
        dict(x=x_prompt, blk_a=(1, 4 * CHUNK), blk_b=(bp, CHUNK),
             conv=jnp.zeros((depth, bp, CONV_WIDTH - 1, CONV_CH), dt),
             rg=jnp.zeros((depth, bp, 1, RG_WIDTH), dt),
             dn=jnp.zeros((depth, bp, DN_HEADS, DN_HEAD_DIM, DN_HEAD_DIM), dt)),
        dict(x=x_sample, blk_a=(4 * CHUNK // ls, ls), blk_b=(4 * CHUNK // ls, ls),
             conv=state_conv, rg=state_rglru.reshape(depth, bs, 1, RG_WIDTH), dn=state_delta),
    ]
    for p in pops:
        p.update(new_conv=jnp.zeros_like(p["conv"]), new_rg=jnp.zeros_like(p["rg"]),
                 new_dn=jnp.zeros_like(p["dn"]))
    pp, ps = pops
    flat = lambda v: v.reshape(-1, D_MODEL)
    for li in range(depth):
        for p in pops:
            p["x1"], prep, p["new_conv"] = _call_a(p["x"], li, p["conv"], p["new_conv"], w,
                                                   ns=p["blk_a"][0], t=p["blk_a"][1])
            p["xm"], p["new_rg"], p["new_dn"] = _call_b(prep, p["x1"], li, p["rg"], p["dn"], p["new_rg"],
                                                        p["new_dn"], w, ns=p["blk_b"][0], t=p["blk_b"][1])
        gfin = final_norm.reshape(1, -1) if li == depth - 1 else None
        y_p, y_s = _call_c(flat(pp["xm"]), flat(ps["xm"]), li, w, gfin)
        pp["x"], ps["x"] = y_p.reshape(pp["x"].shape), y_s.reshape(ps["x"].shape)
    return (pp["x"], ps["x"], pp["new_conv"], pp["new_rg"].reshape(depth, bp, RG_WIDTH), pp["new_dn"],
            ps["new_conv"], ps["new_rg"].reshape(depth, bs, RG_WIDTH), ps["new_dn"])
```
